```python
import math, functools
import jax, jax.numpy as jnp
from jax import lax
import numpy as np

D_MODEL = 2048
BATCH = 1
SEQ = 8192
DEPTH = 1
DEC_BATCH = 32
DEC_SEQ = 1
PAST_LEN = 16384
PAGE_SIZE = 128

N_HEADS = 8
HEAD_DIM = 64
V_DIM = 2 * HEAD_DIM
ATTN_WIDTH = N_HEADS * V_DIM
QK_WIDTH = N_HEADS * 2 * HEAD_DIM
ROT_DIM = HEAD_DIM // 4
ROPE_THETA = 500000.0
Q_BLOCK = 128
S5_WIDTH = D_MODEL - ATTN_WIDTH
S5_CH = 16
S5_GROUPS = S5_WIDTH // S5_CH
S5_STATE = 64
IN_WIDTH = 2 * QK_WIDTH + ATTN_WIDTH + S5_WIDTH
D_FF = 5632
CONV_W = 3
PLE_DIM = 256
EPS = 1e-6

kernel_name = 'hybrid_diffattn_s5_convffn_step'


def _rms(x, g):
    xf = x.astype(jnp.float32)
    y = xf * lax.rsqrt(jnp.mean(xf * xf, axis=-1, keepdims=True) + EPS)
    return (y * g.astype(jnp.float32)).astype(x.dtype)


def _rope(x, pos):
    half = ROT_DIM // 2
    inv = ROPE_THETA ** (-(jnp.arange(half, dtype=jnp.float32) * 2.0 / ROT_DIM))
    ang = pos.astype(jnp.float32)[:, None] * inv[None, :]
    cos = jnp.cos(ang)[:, None, None, :]
    sin = jnp.sin(ang)[:, None, None, :]
    xf = x.astype(jnp.float32)
    x1 = xf[..., :half]
    x2 = xf[..., half:ROT_DIM]
    out = jnp.concatenate([x1 * cos - x2 * sin, x2 * cos + x1 * sin, xf[..., ROT_DIM:]], axis=-1)
    return out.astype(x.dtype)


def _project(a, w_in, q_norm, k_norm, pos):
    B, T, _ = a.shape
    proj = a @ w_in
    q = proj[..., :QK_WIDTH].reshape(B, T, N_HEADS, 2, HEAD_DIM)
    k = proj[..., QK_WIDTH:2 * QK_WIDTH].reshape(B, T, N_HEADS, 2, HEAD_DIM)
    v = proj[..., 2 * QK_WIDTH:2 * QK_WIDTH + ATTN_WIDTH].reshape(B, T, N_HEADS, V_DIM)
    u = proj[..., 2 * QK_WIDTH + ATTN_WIDTH:]
    q = _rope(_rms(q, q_norm), pos)
    k = _rope(_rms(k, k_norm), pos)
    return q, k, v, u


def _diff_lambda(lq1, lk1, lq2, lk2, layer_idx):
    lam_init = 0.8 - 0.6 * math.exp(-0.3 * layer_idx)
    lam = (jnp.exp(jnp.sum(lq1.astype(jnp.float32) * lk1.astype(jnp.float32)))
           - jnp.exp(jnp.sum(lq2.astype(jnp.float32) * lk2.astype(jnp.float32))) + lam_init)
    return lam, lam_init


def _diff_attn_prompt(q, k, v, lam):
    B, S = q.shape[:2]
    nb = S // Q_BLOCK
    scale = HEAD_DIM ** -0.5
    qb = jnp.moveaxis(q.reshape(B, nb, Q_BLOCK, N_HEADS, 2, HEAD_DIM), 1, 0)
    kpos = jnp.arange(S)

    def block(args):
        qi, bi = args
        s = jnp.einsum('bqhcd,bkhcd->bhcqk', qi, k).astype(jnp.float32) * scale
        qpos = bi * Q_BLOCK + jnp.arange(Q_BLOCK)
        mask = kpos[None, :] <= qpos[:, None]
        s = jnp.where(mask, s, -jnp.inf)
        p = jax.nn.softmax(s, axis=-1)
        pd = p[:, :, 0] - lam * p[:, :, 1]
        return jnp.einsum('bhqk,bkhe->bqhe', pd.astype(v.dtype), v)

    out = lax.map(block, (qb, jnp.arange(nb)))
    return jnp.moveaxis(out, 0, 1).reshape(B, S, N_HEADS, V_DIM)


def _diff_attn_sample(q, k, v, cache_k, cache_v, layer_idx, page_table, lam):
    DB, T = q.shape[:2]
    scale = HEAD_DIM ** -0.5
    kp = cache_k[layer_idx, page_table]
    vp = cache_v[layer_idx, page_table]
    L = kp.shape[1] * kp.shape[2]
    kp = kp.reshape(DB, L, N_HEADS, 2, HEAD_DIM)
    vp = vp.reshape(DB, L, N_HEADS, V_DIM)
    s_past = jnp.einsum('bqhcd,bkhcd->bhcqk', q, kp).astype(jnp.float32) * scale
    s_new = jnp.einsum('bqhcd,bkhcd->bhcqk', q, k).astype(jnp.float32) * scale
    tri = jnp.tril(jnp.ones((T, T), dtype=bool))
    s_new = jnp.where(tri, s_new, -jnp.inf)
    p = jax.nn.softmax(jnp.concatenate([s_past, s_new], axis=-1), axis=-1)
    pd = (p[:, :, 0] - lam * p[:, :, 1]).astype(v.dtype)
    return (jnp.einsum('bhqk,bkhe->bqhe', pd[..., :L], vp)
            + jnp.einsum('bhqk,bkhe->bqhe', pd[..., L:], v))


def _s5(u, s0_re, s0_im, a_re, a_im, b_re, b_im, c_re, c_im, d, log_dt, w_glu, b_glu):
    B, T, _ = u.shape
    f32 = jnp.float32
    a_re = a_re.astype(f32); a_im = a_im.astype(f32)
    b_re = b_re.astype(f32); b_im = b_im.astype(f32)
    dt = jnp.exp(log_dt.astype(f32))[:, None]
    mag = jnp.exp(dt * a_re)
    ab_re = mag * jnp.cos(dt * a_im)
    ab_im = mag * jnp.sin(dt * a_im)
    den = a_re * a_re + a_im * a_im
    n_re = ab_re - 1.0
    f_re = (n_re * a_re + ab_im * a_im) / den
    f_im = (ab_im * a_re - n_re * a_im) / den
    bb_re = f_re[..., None] * b_re - f_im[..., None] * b_im
    bb_im = f_re[..., None] * b_im + f_im[..., None] * b_re
    uf = u.astype(f32).reshape(B, T, S5_GROUPS, S5_CH)
    bu_re = jnp.einsum('btgc,gpc->btgp', uf, bb_re)
    bu_im = jnp.einsum('btgc,gpc->btgp', uf, bb_im)
    s0_re = s0_re.astype(f32); s0_im = s0_im.astype(f32)
    bu_re = bu_re.at[:, 0].add(ab_re * s0_re - ab_im * s0_im)
    bu_im = bu_im.at[:, 0].add(ab_re * s0_im + ab_im * s0_re)
    ar = jnp.broadcast_to(ab_re, bu_re.shape)
    ai = jnp.broadcast_to(ab_im, bu_im.shape)

    def combine(e1, e2):
        a1r, a1i, b1r, b1i = e1
        a2r, a2i, b2r, b2i = e2
        return (a2r * a1r - a2i * a1i,
                a2r * a1i + a2i * a1r,
                a2r * b1r - a2i * b1i + b2r,
                a2r * b1i + a2i * b1r + b2i)

    _, _, xr, xi = lax.associative_scan(combine, (ar, ai, bu_re, bu_im), axis=1)
    y = (jnp.einsum('btgp,gcp->btgc', xr, c_re.astype(f32))
         - jnp.einsum('btgp,gcp->btgc', xi, c_im.astype(f32))
         + d.astype(f32) * uf)
    y = y.reshape(B, T, S5_WIDTH)
    g = jax.nn.gelu(y)
    out = g * jax.nn.sigmoid(g @ w_glu.astype(f32) + b_glu.astype(f32))
    return out.astype(u.dtype), xr[:, -1], xi[:, -1]


def _conv_ffn(f, buf, w_gate, w_up, conv_w, conv_b, w_down):
    T = f.shape[1]
    hg = f @ w_gate
    hu = f @ w_up
    ext = jnp.concatenate([buf.astype(hg.dtype), hg], axis=1)
    conv = conv_b + sum(conv_w[j] * ext[:, j:j + T] for j in range(CONV_W))
    out = (jax.nn.gelu(conv) * hu) @ w_down
    return out, ext[:, -(CONV_W - 1):]


def _block_tail(h, att, u, s0_re, s0_im, conv_buf, p_i, lam_init, subln,
                a_re, a_im, b_re, b_im, c_re, c_im, d, log_dt, w_glu, b_glu, w_out,
                norm_ffn, w_gate, w_up, conv_w, conv_b, w_down,
                w_ple_gate, w_ple_proj, ple_norm):
    B, T, _ = h.shape
    att = (_rms(att, subln) * (1.0 - lam_init)).reshape(B, T, ATTN_WIDTH)
    s5o, s_re, s_im = _s5(u, s0_re, s0_im, a_re, a_im, b_re, b_im, c_re, c_im, d, log_dt, w_glu, b_glu)
    h = h + jnp.concatenate([att, s5o.astype(att.dtype)], axis=-1) @ w_out
    ff, new_buf = _conv_ffn(_rms(h, norm_ffn), conv_buf, w_gate, w_up, conv_w, conv_b, w_down)
    h = h + ff
    h = h + jax.nn.sigmoid(h @ w_ple_gate) * _rms(p_i @ w_ple_proj, ple_norm)
    return h, s_re, s_im, new_buf


def setup_inputs(seed: int = 0) -> dict:
    key = jax.random.key(seed)
    ks = iter(jax.random.split(key, 64))
    f32 = jnp.float32

    def nrm(shape, scale):
        return jax.random.normal(next(ks), shape, f32) * scale

    n_pages = PAST_LEN // PAGE_SIZE
    n_used = DEC_BATCH * n_pages
    n_pool = (n_used * 5) // 4
    page_table = jax.random.permutation(next(ks), n_pool)[:n_used].reshape(DEC_BATCH, n_pages).astype(jnp.int32)
    G, P, C = S5_GROUPS, S5_STATE, S5_CH
    return {
        'x_prompt': nrm((BATCH, SEQ, D_MODEL), 1.0),
        'x_sample': nrm((DEC_BATCH, DEC_SEQ, D_MODEL), 1.0),
        'cache_k': nrm((DEPTH, n_pool, PAGE_SIZE, N_HEADS, 2, HEAD_DIM), 1.0),
        'cache_v': nrm((DEPTH, n_pool, PAGE_SIZE, N_HEADS, V_DIM), 1.0),
        'state_s5_re': nrm((DEPTH, DEC_BATCH, G, P), 0.5),
        'state_s5_im': nrm((DEPTH, DEC_BATCH, G, P), 0.5),
        'state_conv': nrm((DEPTH, DEC_BATCH, CONV_W - 1, D_FF), 1.0),
        'page_table': page_table,
        'p_prompt': nrm((DEPTH, BATCH, SEQ, PLE_DIM), 1.0),
        'p_sample': nrm((DEPTH, DEC_BATCH, DEC_SEQ, PLE_DIM), 1.0),
        'norm_mix': 1.0 + nrm((DEPTH, D_MODEL), 0.02),
        'w_in': nrm((DEPTH, D_MODEL, IN_WIDTH), D_MODEL ** -0.5),
        'q_norm': 1.0 + nrm((DEPTH, HEAD_DIM), 0.02),
        'k_norm': 1.0 + nrm((DEPTH, HEAD_DIM), 0.02),
        'lam_q1': nrm((DEPTH, HEAD_DIM), 0.1),
        'lam_k1': nrm((DEPTH, HEAD_DIM), 0.1),
        'lam_q2': nrm((DEPTH, HEAD_DIM), 0.1),
        'lam_k2': nrm((DEPTH, HEAD_DIM), 0.1),
        'subln': 1.0 + nrm((DEPTH, V_DIM), 0.02),
        's5_a_re': -0.5 * jnp.exp(nrm((DEPTH, G, P), 0.02)),
        's5_a_im': jnp.broadcast_to(jnp.pi * jnp.arange(P, dtype=f32), (DEPTH, G, P)) + nrm((DEPTH, G, P), 1e-3),
        's5_b_re': nrm((DEPTH, G, P, C), (2.0 * C) ** -0.5),
        's5_b_im': nrm((DEPTH, G, P, C), (2.0 * C) ** -0.5),
        's5_c_re': nrm((DEPTH, G, C, P), (2.0 * P) ** -0.5),
        's5_c_im': nrm((DEPTH, G, C, P), (2.0 * P) ** -0.5),
        's5_d': nrm((DEPTH, G, C), 1.0),
        's5_log_dt': jax.random.uniform(next(ks), (DEPTH, G), f32, math.log(1e-3), math.log(1e-1)),
        'w_glu': nrm((DEPTH, S5_WIDTH, S5_WIDTH), S5_WIDTH ** -0.5),
        'b_glu': nrm((DEPTH, S5_WIDTH), 0.01),
        'w_out': nrm((DEPTH, D_MODEL, D_MODEL), D_MODEL ** -0.5),
        'norm_ffn': 1.0 + nrm((DEPTH, D_MODEL), 0.02),
        'w_gate': nrm((DEPTH, D_MODEL, D_FF), D_MODEL ** -0.5),
        'w_up': nrm((DEPTH, D_MODEL, D_FF), D_MODEL ** -0.5),
        'conv_w': nrm((DEPTH, CONV_W, D_FF), CONV_W ** -0.5),
        'conv_b': nrm((DEPTH, D_FF), 0.01),
        'w_down': nrm((DEPTH, D_FF, D_MODEL), D_FF ** -0.5),
        'w_ple_gate': nrm((DEPTH, D_MODEL, D_MODEL), D_MODEL ** -0.5),
        'w_ple_proj': nrm((DEPTH, PLE_DIM, D_MODEL), PLE_DIM ** -0.5),
        'ple_norm': 1.0 + nrm((DEPTH, D_MODEL), 0.02),
    }


def reference(x_prompt, x_sample, cache_k, cache_v, state_s5_re, state_s5_im, state_conv, page_table,
              p_prompt, p_sample, norm_mix, w_in, q_norm, k_norm, lam_q1, lam_k1, lam_q2, lam_k2, subln,
              s5_a_re, s5_a_im, s5_b_re, s5_b_im, s5_c_re, s5_c_im, s5_d, s5_log_dt, w_glu, b_glu, w_out,
              norm_ffn, w_gate, w_up, conv_w, conv_b, w_down, w_ple_gate, w_ple_proj, ple_norm):
    B, S, _ = x_prompt.shape
    DB, T, _ = x_sample.shape
    past_len = page_table.shape[1] * PAGE_SIZE
    pos_p = jnp.arange(S)
    pos_s = past_len + jnp.arange(T)
    hp, hs = x_prompt, x_sample
    kp_l, vp_l, srp_l, sip_l, cp_l = [], [], [], [], []
    ks_l, vs_l, srs_l, sis_l, cs_l = [], [], [], [], []
    for i in range(DEPTH):
        lam, lam_init = _diff_lambda(lam_q1[i], lam_k1[i], lam_q2[i], lam_k2[i], i)
        tail = functools.partial(
            _block_tail, lam_init=lam_init, subln=subln[i],
            a_re=s5_a_re[i], a_im=s5_a_im[i], b_re=s5_b_re[i], b_im=s5_b_im[i],
            c_re=s5_c_re[i], c_im=s5_c_im[i], d=s5_d[i], log_dt=s5_log_dt[i],
            w_glu=w_glu[i], b_glu=b_glu[i], w_out=w_out[i], norm_ffn=norm_ffn[i],
            w_gate=w_gate[i], w_up=w_up[i], conv_w=conv_w[i], conv_b=conv_b[i], w_down=w_down[i],
            w_ple_gate=w_ple_gate[i], w_ple_proj=w_ple_proj[i], ple_norm=ple_norm[i])
        q, k, v, u = _project(_rms(hp, norm_mix[i]), w_in[i], q_norm[i], k_norm[i], pos_p)
        att = _diff_attn_prompt(q, k, v, lam)
        z_state = jnp.zeros((B, S5_GROUPS, S5_STATE), jnp.float32)
        z_conv = jnp.zeros((B, CONV_W - 1, D_FF), jnp.float32)
        hp, s_re, s_im, cbuf = tail(hp, att, u, z_state, z_state, z_conv, p_prompt[i])
        kp_l.append(k); vp_l.append(v); srp_l.append(s_re); sip_l.append(s_im); cp_l.append(cbuf)
        q, k, v, u = _project(_rms(hs, norm_mix[i]), w_in[i], q_norm[i], k_norm[i], pos_s)
        att = _diff_attn_sample(q, k, v, cache_k, cache_v, i, page_table, lam)
        hs, s_re, s_im, cbuf = tail(hs, att, u, state_s5_re[i], state_s5_im[i], state_conv[i], p_sample[i])
        ks_l.append(k); vs_l.append(v); srs_l.append(s_re); sis_l.append(s_im); cs_l.append(cbuf)
    return (hp, hs,
            jnp.stack(kp_l), jnp.stack(vp_l), jnp.stack(srp_l), jnp.stack(sip_l), jnp.stack(cp_l),
            jnp.stack(ks_l), jnp.stack(vs_l), jnp.stack(srs_l), jnp.stack(sis_l), jnp.stack(cs_l))
```

```python
import functools
import math

import jax
import jax.numpy as jnp
from jax import lax
from jax.experimental import pallas as pl
from jax.experimental.pallas import tpu as pltpu

N_HEADS = 8
HEAD_DIM = 64
V_DIM = 2 * HEAD_DIM
ROT_DIM = HEAD_DIM // 4
ROPE_THETA = 500000.0
S5_CH = 16
S5_STATE = 64
CONV_W = 3
EPS = 1e-6

V7X_LANES = 128
V7X_SUBLANES = 8
VMEM_LIMIT_BYTES = 56 * 1024 * 1024

S5_SLAB_GROUPS = V7X_LANES // S5_CH
S5_SLAB_STATE = S5_SLAB_GROUPS * S5_STATE

F32 = jnp.float32
BF16 = jnp.bfloat16


def _tile(n, pref):
    t = min(n, pref)
    assert n % t == 0, (n, t)
    return t


def _params(sem, vmem=VMEM_LIMIT_BYTES):
    return pltpu.CompilerParams(dimension_semantics=sem, vmem_limit_bytes=vmem)


def _qk_post(p, gain, cos, sin_lo, sin_hi, seg):
    outs = []
    for c in range(p.shape[1] // V7X_LANES):
        pc = p[:, c * V7X_LANES:(c + 1) * V7X_LANES]
        ms = jnp.dot((pc * pc).astype(BF16), seg, preferred_element_type=F32)
        y = pc * lax.rsqrt(ms + EPS) * gain
        y = (y * cos
             + pltpu.roll(y, V7X_LANES - ROT_DIM // 2, 1) * sin_lo
             + pltpu.roll(y, ROT_DIM // 2, 1) * sin_hi)
        outs.append(y)
    return jnp.concatenate(outs, axis=1)


def _proj_kernel(x_ref, g_ref, w_ref, qg_ref, kg_ref, cos_ref, slo_ref, shi_ref, seg_ref,
                 q_ref, k32_ref, kb_ref, v32_ref, vb_ref, u_ref, xn_ref):
    j = pl.program_id(1)

    @pl.when(j == 0)
    def _():
        x = x_ref[...]
        ms = jnp.mean(x * x, axis=-1, keepdims=True)
        xn_ref[...] = (x * lax.rsqrt(ms + EPS) * g_ref[...]).astype(BF16)

    proj = jnp.dot(xn_ref[...], w_ref[...], preferred_element_type=F32)

    @pl.when(j == 0)
    def _():
        q = _qk_post(proj, qg_ref[...], cos_ref[...], slo_ref[...], shi_ref[...], seg_ref[...])
        q_ref[...] = (q * (HEAD_DIM ** -0.5)).astype(BF16)

    @pl.when(j == 1)
    def _():
        k = _qk_post(proj, kg_ref[...], cos_ref[...], slo_ref[...], shi_ref[...], seg_ref[...])
        k32_ref[...] = k
        kb_ref[...] = k.astype(BF16)

    @pl.when(j == 2)
    def _():
        v32_ref[...] = proj
        vb_ref[...] = proj.astype(BF16)

    @pl.when(j == 3)
    def _():
        u_ref[...] = proj


def _rope_tables(pos):
    half = ROT_DIM // 2
    inv = ROPE_THETA ** (-(jnp.arange(half, dtype=F32) * 2.0 / ROT_DIM))
    ang = pos.astype(F32)[:, None] * inv[None, :]
    cos, sin = jnp.cos(ang), jnp.sin(ang)
    t = pos.shape[0]
    ones = jnp.ones((t, HEAD_DIM - ROT_DIM), F32)
    zeros = jnp.zeros((t, HEAD_DIM - ROT_DIM), F32)
    zh = jnp.zeros((t, half), F32)
    cos_t = jnp.concatenate([cos, cos, ones], axis=1)
    lo_t = jnp.concatenate([-sin, zh, zeros], axis=1)
    hi_t = jnp.concatenate([zh, sin, zeros], axis=1)
    rep = V7X_LANES // HEAD_DIM
    return (jnp.tile(cos_t, (1, rep)), jnp.tile(lo_t, (1, rep)), jnp.tile(hi_t, (1, rep)))


def _project(x, pos, norm_mix, w_in_b, q_norm, k_norm, tm):
    m, d = x.shape
    width = w_in_b.shape[1] // 4
    tm = _tile(m, tm)
    cos_t, lo_t, hi_t = _rope_tables(pos)
    rep = V7X_LANES // HEAD_DIM
    qg = jnp.tile(q_norm.reshape(1, HEAD_DIM), (1, rep))
    kg = jnp.tile(k_norm.reshape(1, HEAD_DIM), (1, rep))
    lane = jnp.arange(V7X_LANES)
    seg = ((lane[:, None] // HEAD_DIM) == (lane[None, :] // HEAD_DIM)).astype(BF16) / HEAD_DIM
    row = lambda i, j: (i, 0)
    const = lambda i, j: (0, 0)
    out_spec = pl.BlockSpec((tm, width), row)
    return pl.pallas_call(
        _proj_kernel,
        grid=(m // tm, 4),
        in_specs=[
            pl.BlockSpec((tm, d), row),
            pl.BlockSpec((1, d), const),
            pl.BlockSpec((d, width), lambda i, j: (0, j)),
            pl.BlockSpec((1, V7X_LANES), const),
            pl.BlockSpec((1, V7X_LANES), const),
            pl.BlockSpec((tm, V7X_LANES), row),
            pl.BlockSpec((tm, V7X_LANES), row),
            pl.BlockSpec((tm, V7X_LANES), row),
            pl.BlockSpec((V7X_LANES, V7X_LANES), const),
        ],
        out_specs=[out_spec] * 6,
        out_shape=[
            jax.ShapeDtypeStruct((m, width), BF16),
            jax.ShapeDtypeStruct((m, width), F32),
            jax.ShapeDtypeStruct((m, width), BF16),
            jax.ShapeDtypeStruct((m, width), F32),
            jax.ShapeDtypeStruct((m, width), BF16),
            jax.ShapeDtypeStruct((m, width), F32),
        ],
        scratch_shapes=[pltpu.VMEM((tm, d), BF16)],
        compiler_params=_params(("parallel", "arbitrary")),
        name="in_proj",
    )(x, norm_mix.reshape(1, d), w_in_b, qg, kg, cos_t, lo_t, hi_t, seg)


def _lambda(lam_ref, lam_init):
    lv = lam_ref[...]
    l1 = jnp.sum(lv[0:1] * lv[1:2], axis=-1, keepdims=True)
    l2 = jnp.sum(lv[2:3] * lv[3:4], axis=-1, keepdims=True)
    return jnp.exp(l1) - jnp.exp(l2) + lam_init


def _subln(out, sub, lam_init):
    ms = jnp.mean(out * out, axis=-1, keepdims=True)
    return out * lax.rsqrt(ms + EPS) * sub * (1.0 - lam_init)


def _attn_kernel(q_ref, k_ref, v_ref, lam_ref, sub_ref, o_ref, m_ref, l_ref, acc_ref,
                 *, tq, lam_init):
    qi = pl.program_id(1)
    q = q_ref[...].astype(F32)
    lane = lax.broadcasted_iota(jnp.int32, q.shape, 1)
    qs = jnp.concatenate([jnp.where(lane < HEAD_DIM, q, 0.0),
                          jnp.where(lane >= HEAD_DIM, q, 0.0)], axis=0).astype(BF16)

    m_ref[...] = jnp.full(m_ref.shape, -jnp.inf, F32)
    l_ref[...] = jnp.zeros(l_ref.shape, F32)
    acc_ref[...] = jnp.zeros(acc_ref.shape, F32)

    def step(ki, masked):
        k0 = pl.multiple_of(ki * tq, tq)
        kb = k_ref[pl.ds(k0, tq), :]
        vb = v_ref[pl.ds(k0, tq), :]
        s = lax.dot_general(qs, kb, (((1,), (1,)), ((), ())), preferred_element_type=F32)
        if masked:
            r = lax.broadcasted_iota(jnp.int32, s.shape, 0)
            c = lax.broadcasted_iota(jnp.int32, s.shape, 1)
            r = jnp.where(r >= tq, r - tq, r)
            s = jnp.where(c <= r, s, -jnp.inf)
        m_prev = m_ref[...]
        m_new = jnp.maximum(m_prev, jnp.max(s, axis=1, keepdims=True))
        alpha = jnp.exp(m_prev - m_new)
        p = jnp.exp(s - m_new)
        l_ref[...] = alpha * l_ref[...] + jnp.sum(p, axis=1, keepdims=True)
        acc_ref[...] = alpha * acc_ref[...] + jnp.dot(p.astype(BF16), vb,
                                                      preferred_element_type=F32)
        m_ref[...] = m_new

    def body(ki, carry):
        step(ki, False)
        return carry

    lax.fori_loop(0, qi, body, 0)
    step(qi, True)

    lam = _lambda(lam_ref, lam_init)
    o = acc_ref[...] / l_ref[...]
    out = o[:tq] - lam * o[tq:]
    o_ref[...] = _subln(out, sub_ref[...], lam_init).astype(o_ref.dtype)


def _attn_prompt(q, k, v, lam_vec, subln, lam_init, tq):
    s, width = q.shape
    tq = _tile(s, tq)
    kern = functools.partial(_attn_kernel, tq=tq, lam_init=lam_init)
    return pl.pallas_call(
        kern,
        grid=(N_HEADS, s // tq),
        in_specs=[
            pl.BlockSpec((tq, V_DIM), lambda h, i: (i, h)),
            pl.BlockSpec((s, V_DIM), lambda h, i: (0, h)),
            pl.BlockSpec((s, V_DIM), lambda h, i: (0, h)),
            pl.BlockSpec((4, HEAD_DIM), lambda h, i: (0, 0)),
            pl.BlockSpec((1, V_DIM), lambda h, i: (0, 0)),
        ],
        out_specs=pl.BlockSpec((tq, V_DIM), lambda h, i: (i, h)),
        out_shape=jax.ShapeDtypeStruct((s, width), BF16),
        scratch_shapes=[pltpu.VMEM((2 * tq, 1), F32), pltpu.VMEM((2 * tq, 1), F32),
                        pltpu.VMEM((2 * tq, V_DIM), F32)],
        compiler_params=_params(("parallel", "arbitrary")),
        name="attn_prompt",
    )(q, k, v, lam_vec, subln.reshape(1, V_DIM))


def _attn_sample_kernel(pt_ref, *refs, n_pg, lam_init):
    k_refs = refs[:n_pg]
    v_refs = refs[n_pg:2 * n_pg]
    q_ref, kn_ref, vn_ref, lam_ref, sub_ref, o_ref, qm_ref, m_ref, l_ref, acc_ref = refs[2 * n_pg:]
    s_id = pl.program_id(1)
    n_rows = 2 * N_HEADS
    width = q_ref.shape[-1]

    @pl.when(s_id == 0)
    def _():
        q = jnp.broadcast_to(q_ref[0].astype(F32), (n_rows, width))
        r = lax.broadcasted_iota(jnp.int32, (n_rows, width), 0)
        c = lax.broadcasted_iota(jnp.int32, (n_rows, width), 1)
        keep = (c >= r * HEAD_DIM) & (c < (r + 1) * HEAD_DIM)
        qm_ref[...] = jnp.where(keep, q, 0.0).astype(BF16)
        m_ref[...] = jnp.full(m_ref.shape, -jnp.inf, F32)
        l_ref[...] = jnp.zeros(l_ref.shape, F32)
        acc_ref[...] = jnp.zeros(acc_ref.shape, F32)

    qm = qm_ref[...]
    s = jnp.concatenate(
        [lax.dot_general(qm, k_refs[g][...].astype(BF16), (((1,), (1,)), ((), ())),
                         preferred_element_type=F32) for g in range(n_pg)], axis=1)
    m_prev = m_ref[...]
    m_new = jnp.maximum(m_prev, jnp.max(s, axis=1, keepdims=True))
    alpha = jnp.exp(m_prev - m_new)
    p = jnp.exp(s - m_new)
    l_ref[...] = alpha * l_ref[...] + jnp.sum(p, axis=1, keepdims=True)
    pb = p.astype(BF16)
    page = k_refs[0].shape[0]
    acc = alpha * acc_ref[...]
    for g in range(n_pg):
        acc = acc + jnp.dot(pb[:, g * page:(g + 1) * page], v_refs[g][...].astype(BF16),
                            preferred_element_type=F32)
    acc_ref[...] = acc
    m_ref[...] = m_new

    @pl.when(s_id == pl.num_programs(1) - 1)
    def _():
        kn = kn_ref[0].astype(F32)
        vn = vn_ref[0].astype(F32)
        s_new = jnp.sum(qm.astype(F32) * kn, axis=1, keepdims=True)
        m_old = m_ref[...]
        m_fin = jnp.maximum(m_old, s_new)
        a = jnp.exp(m_old - m_fin)
        p_new = jnp.exp(s_new - m_fin)
        l_fin = a * l_ref[...] + p_new
        acc_fin = a * acc_ref[...] + p_new.astype(BF16).astype(F32) * vn
        o = acc_fin / l_fin
        o1 = jnp.concatenate([o[2 * h:2 * h + 1, h * V_DIM:(h + 1) * V_DIM]
                              for h in range(N_HEADS)], axis=0)
        o2 = jnp.concatenate([o[2 * h + 1:2 * h + 2, h * V_DIM:(h + 1) * V_DIM]
                              for h in range(N_HEADS)], axis=0)
        lam = _lambda(lam_ref, lam_init)
        o_ref[0] = _subln(o1 - lam * o2, sub_ref[...], lam_init)


def _attn_sample(q, k_new, v_new, cache_k, cache_v, page_table, lam_vec, subln, lam_init, n_pg):
    db, width = q.shape
    n_pool, page = cache_k.shape[0], cache_k.shape[1]
    n_pages = page_table.shape[1]
    n_pg = _tile(n_pages, n_pg)
    ck = cache_k.reshape(n_pool, page, width)
    cv = cache_v.reshape(n_pool, page, width)

    def page_spec(g):
        return pl.BlockSpec((None, page, width),
                            lambda b, s, pt, g=g: (pt[b, s * n_pg + g], 0, 0))

    tok = pl.BlockSpec((1, 1, width), lambda b, s, pt: (b, 0, 0))
    kern = functools.partial(_attn_sample_kernel, n_pg=n_pg, lam_init=lam_init)
    n_rows = 2 * N_HEADS
    out = pl.pallas_call(
        kern,
        grid_spec=pltpu.PrefetchScalarGridSpec(
            num_scalar_prefetch=1,
            grid=(db, n_pages // n_pg),
            in_specs=[page_spec(g) for g in range(n_pg)] * 2 + [
                tok, tok, tok,
                pl.BlockSpec((4, HEAD_DIM), lambda b, s, pt: (0, 0)),
                pl.BlockSpec((1, V_DIM), lambda b, s, pt: (0, 0)),
            ],
            out_specs=pl.BlockSpec((1, N_HEADS, V_DIM), lambda b, s, pt: (b, 0, 0)),
            scratch_shapes=[pltpu.VMEM((n_rows, width), BF16),
                            pltpu.VMEM((n_rows, 1), F32), pltpu.VMEM((n_rows, 1), F32),
                            pltpu.VMEM((n_rows, width), F32)],
        ),
        out_shape=jax.ShapeDtypeStruct((db, N_HEADS, V_DIM), F32),
        compiler_params=_params(("parallel", "arbitrary")),
        name="attn_sample",
    )(page_table, *([ck] * n_pg), *([cv] * n_pg),
      q.reshape(db, 1, width), k_new.reshape(db, 1, width), v_new.reshape(db, 1, width),
      lam_vec, subln.reshape(1, V_DIM))
    return out.reshape(db, width).astype(BF16)


def _s5_tables(a_re, a_im, b_re, b_im, c_re, c_im, d, log_dt):
    g, p = a_re.shape
    n_slab = g // S5_SLAB_GROUPS
    dt = jnp.exp(log_dt.astype(F32))[:, None]
    mag = jnp.exp(dt * a_re)
    ab_re = mag * jnp.cos(dt * a_im)
    ab_im = mag * jnp.sin(dt * a_im)
    den = a_re * a_re + a_im * a_im
    n_re = ab_re - 1.0
    f_re = (n_re * a_re + ab_im * a_im) / den
    f_im = (ab_im * a_re - n_re * a_im) / den
    bb_re = f_re[..., None] * b_re - f_im[..., None] * b_im
    bb_im = f_re[..., None] * b_im + f_im[..., None] * b_re
    eye = jnp.eye(S5_SLAB_GROUPS, dtype=F32)

    def in_mat(bb):
        bb = bb.reshape(n_slab, S5_SLAB_GROUPS, p, S5_CH)
        return jnp.einsum('jgpc,gh->jgchp', bb, eye).reshape(n_slab, V7X_LANES, S5_SLAB_STATE)

    def out_mat(cc):
        cc = cc.reshape(n_slab, S5_SLAB_GROUPS, S5_CH, p)
        return jnp.einsum('jgcp,gh->jgphc', cc, eye).reshape(n_slab, S5_SLAB_STATE, V7X_LANES)

    w_in = jnp.concatenate([in_mat(bb_re), in_mat(bb_im)], axis=2).astype(BF16)
    w_out = jnp.concatenate([out_mat(c_re.astype(F32)), out_mat(-c_im.astype(F32))],
                            axis=1).astype(BF16)

    pw_re, pw_im = [ab_re], [ab_im]
    for _ in range(V7X_SUBLANES - 1):
        r, i = pw_re[-1], pw_im[-1]
        pw_re.append(r * ab_re - i * ab_im)
        pw_im.append(r * ab_im + i * ab_re)
    flat = lambda t: t.reshape(n_slab, 1, S5_SLAB_STATE)
    p_re = jnp.concatenate([flat(t) for t in pw_re], axis=1)
    p_im = jnp.concatenate([flat(t) for t in pw_im], axis=1)
    rows = jnp.arange(V7X_SUBLANES)[None, :, None]
    steps = []
    for k in (1, 2, 4):
        steps.append(jnp.where(rows >= k, flat(pw_re[k - 1]), 0.0))
        steps.append(jnp.where(rows >= k, flat(pw_im[k - 1]), 0.0))
    scan = jnp.stack([p_re, p_im] + steps, axis=1)
    dvec = d.astype(F32).reshape(n_slab, 1, V7X_LANES)
    return w_in, w_out, scan, dvec


def _glu(y, wglu_ref, bglu_ref):
    g = jax.nn.gelu(y)
    z = jnp.dot(g.astype(BF16), wglu_ref[...], preferred_element_type=F32) + bglu_ref[...]
    return g * jax.nn.sigmoid(z)


def _s5_prompt_kernel(u_ref, win_ref, wout_ref, scan_ref, d_ref, wglu_ref, bglu_ref,
                      o_ref, st_ref, x_scr, y_scr, c_scr, *, tt):
    t = pl.program_id(0)
    n_slab = win_ref.shape[0]
    ns = S5_SLAB_STATE

    @pl.when(t == 0)
    def _():
        c_scr[...] = jnp.zeros(c_scr.shape, F32)

    for j in range(n_slab):
        lanes = slice(j * V7X_LANES, (j + 1) * V7X_LANES)
        u_j = u_ref[:, lanes]
        x_scr[...] = jnp.dot(u_j.astype(BF16), win_ref[j], preferred_element_type=F32)
        p_re, p_im = scan_ref[j, 0], scan_ref[j, 1]

        def body(i, carry):
            cr, ci = carry
            r0 = pl.multiple_of(i * V7X_SUBLANES, V7X_SUBLANES)
            xr = x_scr[pl.ds(r0, V7X_SUBLANES), 0:ns]
            xi = x_scr[pl.ds(r0, V7X_SUBLANES), ns:2 * ns]
            for n, k in enumerate((1, 2, 4)):
                mr, mi = scan_ref[j, 2 + 2 * n], scan_ref[j, 3 + 2 * n]
                sr = pltpu.roll(xr, k, 0)
                si = pltpu.roll(xi, k, 0)
                xr, xi = xr + mr * sr - mi * si, xi + mr * si + mi * sr
            xr, xi = xr + p_re * cr - p_im * ci, xi + p_re * ci + p_im * cr
            x_scr[pl.ds(r0, V7X_SUBLANES), 0:ns] = xr
            x_scr[pl.ds(r0, V7X_SUBLANES), ns:2 * ns] = xi
            return (xr[V7X_SUBLANES - 1:V7X_SUBLANES], xi[V7X_SUBLANES - 1:V7X_SUBLANES])

        cr, ci = lax.fori_loop(0, tt // V7X_SUBLANES, body,
                               (c_scr[j:j + 1, 0:ns], c_scr[j:j + 1, ns:2 * ns]))
        c_scr[j:j + 1, 0:ns] = cr
        c_scr[j:j + 1, ns:2 * ns] = ci
        y = jnp.dot(x_scr[...].astype(BF16), wout_ref[j], preferred_element_type=F32)
        y_scr[:, lanes] = y + d_ref[j] * u_j

    o_ref[...] = _glu(y_scr[...], wglu_ref, bglu_ref).astype(o_ref.dtype)
    st_ref[...] = c_scr[...]


def _s5_prompt(u, tabs, w_glu_b, b_glu, tt):
    s, width = u.shape
    w_in, w_out, scan, dvec = tabs
    n_slab = w_in.shape[0]
    tt = _tile(s, tt)
    full = lambda a: pl.BlockSpec(a.shape, lambda t: (0,) * a.ndim)
    bglu = b_glu.reshape(1, width).astype(F32)
    kern = functools.partial(_s5_prompt_kernel, tt=tt)
    out, st = pl.pallas_call(
        kern,
        grid=(s // tt,),
        in_specs=[pl.BlockSpec((tt, width), lambda t: (t, 0)),
                  full(w_in), full(w_out), full(scan), full(dvec), full(w_glu_b), full(bglu)],
        out_specs=[pl.BlockSpec((tt, width), lambda t: (t, 0)),
                   pl.BlockSpec((n_slab, 2 * S5_SLAB_STATE), lambda t: (0, 0))],
        out_shape=[jax.ShapeDtypeStruct((s, width), BF16),
                   jax.ShapeDtypeStruct((n_slab, 2 * S5_SLAB_STATE), F32)],
        scratch_shapes=[pltpu.VMEM((tt, 2 * S5_SLAB_STATE), F32),
                        pltpu.VMEM((tt, width), F32),
                        pltpu.VMEM((n_slab, 2 * S5_SLAB_STATE), F32)],
        compiler_params=_params(("arbitrary",)),
        name="s5_prompt",
    )(u, w_in, w_out, scan, dvec, w_glu_b, bglu)
    groups = n_slab * S5_SLAB_GROUPS
    s_re = st[:, :S5_SLAB_STATE].reshape(groups, S5_STATE)
    s_im = st[:, S5_SLAB_STATE:].reshape(groups, S5_STATE)
    return out, s_re, s_im


def _s5_sample_kernel(u_ref, sre_ref, sim_ref, win_ref, wout_ref, scan_ref, d_ref, wglu_ref,
                      bglu_ref, o_ref, xre_ref, xim_ref, y_scr):
    n_slab = win_ref.shape[0]
    ns = S5_SLAB_STATE
    for j in range(n_slab):
        lanes = slice(j * V7X_LANES, (j + 1) * V7X_LANES)
        st = slice(j * ns, (j + 1) * ns)
        u_j = u_ref[:, lanes]
        bu = jnp.dot(u_j.astype(BF16), win_ref[j], preferred_element_type=F32)
        ab_re, ab_im = scan_ref[j, 0, 0:1], scan_ref[j, 1, 0:1]
        s_re, s_im = sre_ref[:, st], sim_ref[:, st]
        xr = ab_re * s_re - ab_im * s_im + bu[:, 0:ns]
        xi = ab_re * s_im + ab_im * s_re + bu[:, ns:2 * ns]
        xre_ref[:, st] = xr
        xim_ref[:, st] = xi
        x = jnp.concatenate([xr, xi], axis=1).astype(BF16)
        y = jnp.dot(x, wout_ref[j], preferred_element_type=F32)
        y_scr[:, lanes] = y + d_ref[j] * u_j
    o_ref[...] = _glu(y_scr[...], wglu_ref, bglu_ref).astype(o_ref.dtype)


def _s5_sample(u, s_re, s_im, tabs, w_glu_b, b_glu):
    db, width = u.shape
    w_in, w_out, scan, dvec = tabs
    n_state = s_re.shape[1] * s_re.shape[2]
    bglu = b_glu.reshape(1, width).astype(F32)
    args = (u, s_re.reshape(db, n_state).astype(F32), s_im.reshape(db, n_state).astype(F32),
            w_in, w_out, scan, dvec, w_glu_b, bglu)
    full = lambda a: pl.BlockSpec(a.shape, lambda i: (0,) * a.ndim)
    out, x_re, x_im = pl.pallas_call(
        _s5_sample_kernel,
        grid=(1,),
        in_specs=[full(a) for a in args],
        out_specs=[pl.BlockSpec((db, width), lambda i: (0, 0)),
                   pl.BlockSpec((db, n_state), lambda i: (0, 0)),
                   pl.BlockSpec((db, n_state), lambda i: (0, 0))],
        out_shape=[jax.ShapeDtypeStruct((db, width), BF16),
                   jax.ShapeDtypeStruct((db, n_state), F32),
                   jax.ShapeDtypeStruct((db, n_state), F32)],
        scratch_shapes=[pltpu.VMEM((db, width), F32)],
        compiler_params=_params(("arbitrary",)),
        name="s5_sample",
    )(*args)
    return out, x_re.reshape(s_re.shape), x_im.reshape(s_im.shape)


def _outproj_kernel(x_ref, a_ref, s_ref, w_ref, g_ref, h_ref, f_ref):
    half = a_ref.shape[1]
    h = (x_ref[...]
         + jnp.dot(a_ref[...], w_ref[0:half, :], preferred_element_type=F32)
         + jnp.dot(s_ref[...], w_ref[half:, :], preferred_element_type=F32))
    h_ref[...] = h
    ms = jnp.mean(h * h, axis=-1, keepdims=True)
    f_ref[...] = (h * lax.rsqrt(ms + EPS) * g_ref[...]).astype(BF16)


def _outproj(x, att, s5o, w_out_b, norm_ffn, tm):
    m, d = x.shape
    half = att.shape[1]
    tm = _tile(m, tm)
    row = lambda i: (i, 0)
    const = lambda i: (0, 0)
    return pl.pallas_call(
        _outproj_kernel,
        grid=(m // tm,),
        in_specs=[pl.BlockSpec((tm, d), row), pl.BlockSpec((tm, half), row),
                  pl.BlockSpec((tm, half), row), pl.BlockSpec((d, d), const),
                  pl.BlockSpec((1, d), const)],
        out_specs=[pl.BlockSpec((tm, d), row), pl.BlockSpec((tm, d), row)],
        out_shape=[jax.ShapeDtypeStruct((m, d), F32), jax.ShapeDtypeStruct((m, d), BF16)],
        compiler_params=_params(("parallel",)),
        name="out_proj",
    )(x, att, s5o, w_out_b, norm_ffn.reshape(1, d))


def _ffn_tail(conv, hu, wd_ref, h_ref, o_ref, acc_ref):
    j = pl.program_id(1)
    part = jnp.dot((jax.nn.gelu(conv) * hu).astype(BF16), wd_ref[...],
                   preferred_element_type=F32)

    @pl.when(j == 0)
    def _():
        acc_ref[...] = h_ref[...] + part

    @pl.when(j > 0)
    def _():
        acc_ref[...] += part

    @pl.when(j == pl.num_programs(1) - 1)
    def _():
        o_ref[...] = acc_ref[...]


def _ffn_prompt_kernel(f_ref, h_ref, wg_ref, wu_ref, wd_ref, cw_ref, cb_ref,
                       o_ref, buf_ref, acc_ref, hgx_ref, prev_ref, *, tm):
    i = pl.program_id(0)
    j = pl.program_id(1)
    halo = V7X_SUBLANES
    f = f_ref[...]
    hg = jnp.dot(f, wg_ref[...], preferred_element_type=F32)
    hu = jnp.dot(f, wu_ref[...], preferred_element_type=F32)

    @pl.when(i == 0)
    def _():
        hgx_ref[0:halo, :] = jnp.zeros((halo, hg.shape[1]), F32)

    @pl.when(i > 0)
    def _():
        hgx_ref[0:halo, :] = prev_ref[j]

    hgx_ref[halo:, :] = hg
    tail = hg[tm - halo:, :]
    prev_ref[j] = tail
    buf_ref[...] = tail
    cw = cw_ref[...]
    conv = cb_ref[...] + cw[2:3] * hg
    for tap in range(CONV_W - 1):
        off = halo - (CONV_W - 1) + tap
        conv = conv + cw[tap:tap + 1] * hgx_ref[pl.ds(off, tm), :]
    _ffn_tail(conv, hu, wd_ref, h_ref, o_ref, acc_ref)


def _ffn_sample_kernel(f_ref, h_ref, wg_ref, wu_ref, wd_ref, cw_ref, cb_ref, b0_ref, b1_ref,
                       o_ref, hg_ref, acc_ref):
    f = f_ref[...]
    hg = jnp.dot(f, wg_ref[...], preferred_element_type=F32)
    hu = jnp.dot(f, wu_ref[...], preferred_element_type=F32)
    hg_ref[...] = hg
    cw = cw_ref[...]
    conv = cb_ref[...] + cw[0:1] * b0_ref[...] + cw[1:2] * b1_ref[...] + cw[2:3] * hg
    _ffn_tail(conv, hu, wd_ref, h_ref, o_ref, acc_ref)


def _ffn(f, h, w_gate_b, w_up_b, w_down_b, conv_w, conv_b, conv_buf, tm, tf):
    m, d = f.shape
    dff = w_gate_b.shape[1]
    tm = _tile(m, tm)
    tf = _tile(dff, tf)
    n_ff = dff // tf
    row = lambda i, j: (i, 0)
    col = lambda i, j: (0, j)
    common_in = [pl.BlockSpec((tm, d), row), pl.BlockSpec((tm, d), row),
                 pl.BlockSpec((d, tf), col), pl.BlockSpec((d, tf), col),
                 pl.BlockSpec((tf, d), lambda i, j: (j, 0)),
                 pl.BlockSpec((CONV_W, tf), col), pl.BlockSpec((1, tf), col)]
    common_args = (f, h, w_gate_b, w_up_b, w_down_b, conv_w.astype(F32),
                   conv_b.reshape(1, dff).astype(F32))
    acc = pltpu.VMEM((tm, d), F32)
    if conv_buf is None:
        halo = V7X_SUBLANES
        kern = functools.partial(_ffn_prompt_kernel, tm=tm)
        out, tail = pl.pallas_call(
            kern,
            grid=(m // tm, n_ff),
            in_specs=common_in,
            out_specs=[pl.BlockSpec((tm, d), row), pl.BlockSpec((halo, tf), col)],
            out_shape=[jax.ShapeDtypeStruct((m, d), F32),
                       jax.ShapeDtypeStruct((halo, dff), F32)],
            scratch_shapes=[acc, pltpu.VMEM((tm + halo, tf), F32),
                            pltpu.VMEM((n_ff, halo, tf), F32)],
            compiler_params=_params(("arbitrary", "arbitrary")),
            name="ffn_prompt",
        )(*common_args)
        return out, tail[halo - (CONV_W - 1):]
    b0 = conv_buf[:, 0, :].astype(F32)
    b1 = conv_buf[:, 1, :].astype(F32)
    out, hg = pl.pallas_call(
        _ffn_sample_kernel,
        grid=(m // tm, n_ff),
        in_specs=common_in + [pl.BlockSpec((tm, tf), lambda i, j: (i, j))] * 2,
        out_specs=[pl.BlockSpec((tm, d), row), pl.BlockSpec((tm, tf), lambda i, j: (i, j))],
        out_shape=[jax.ShapeDtypeStruct((m, d), F32), jax.ShapeDtypeStruct((m, dff), F32)],
        scratch_shapes=[acc],
        compiler_params=_params(("parallel", "arbitrary")),
        name="ffn_sample",
    )(*common_args, b0, b1)
    return out, jnp.stack([b1, hg], axis=1)


def _ple_kernel(h_ref, p_ref, wg_ref, wp_ref, g_ref, o_ref):
    h = h_ref[...]
    gate = jax.nn.sigmoid(jnp.dot(h.astype(BF16), wg_ref[...], preferred_element_type=F32))
    e = jnp.dot(p_ref[...].astype(BF16), wp_ref[...], preferred_element_type=F32)
    ms = jnp.mean(e * e, axis=-1, keepdims=True)
    o_ref[...] = h + gate * (e * lax.rsqrt(ms + EPS) * g_ref[...])


def _ple(h, p, w_gate_b, w_proj_b, ple_norm, tm):
    m, d = h.shape
    pd = p.shape[1]
    tm = _tile(m, tm)
    row = lambda i: (i, 0)
    const = lambda i: (0, 0)
    return pl.pallas_call(
        _ple_kernel,
        grid=(m // tm,),
        in_specs=[pl.BlockSpec((tm, d), row), pl.BlockSpec((tm, pd), row),
                  pl.BlockSpec((d, d), const), pl.BlockSpec((pd, d), const),
                  pl.BlockSpec((1, d), const)],
        out_specs=pl.BlockSpec((tm, d), row),
        out_shape=jax.ShapeDtypeStruct((m, d), F32),
        compiler_params=_params(("parallel",)),
        name="ple_gate",
    )(h, p, w_gate_b, w_proj_b, ple_norm.reshape(1, d))


def kernel(x_prompt, x_sample, cache_k, cache_v, state_s5_re, state_s5_im, state_conv, page_table, p_prompt, p_sample, norm_mix, w_in, q_norm, k_norm, lam_q1, lam_k1, lam_q2, lam_k2, subln, s5_a_re, s5_a_im, s5_b_re, s5_b_im, s5_c_re, s5_c_im, s5_d, s5_log_dt, w_glu, b_glu, w_out, norm_ffn, w_gate, w_up, conv_w, conv_b, w_down, w_ple_gate, w_ple_proj, ple_norm):
    depth = w_in.shape[0]
    b, s, d = x_prompt.shape
    db, t_new, _ = x_sample.shape
    assert b == 1 and t_new == 1
    page = cache_k.shape[2]
    past_len = page_table.shape[1] * page
    hp = x_prompt.reshape(s, d)
    hs = x_sample.reshape(db, d)
    pos_p = jnp.arange(s)
    pos_s = jnp.full((db,), past_len)
    outs = [[] for _ in range(10)]
    for i in range(depth):
        lam_init = 0.8 - 0.6 * math.exp(-0.3 * i)
        lam_vec = jnp.stack([lam_q1[i], lam_k1[i], lam_q2[i], lam_k2[i]]).astype(F32)
        w_in_b = w_in[i].astype(BF16)
        w_glu_b = w_glu[i].astype(BF16)
        w_out_b = w_out[i].astype(BF16)
        w_gate_b = w_gate[i].astype(BF16)
        w_up_b = w_up[i].astype(BF16)
        w_down_b = w_down[i].astype(BF16)
        w_pg_b = w_ple_gate[i].astype(BF16)
        w_pp_b = w_ple_proj[i].astype(BF16)
        tabs = _s5_tables(s5_a_re[i], s5_a_im[i], s5_b_re[i], s5_b_im[i], s5_c_re[i],
                          s5_c_im[i], s5_d[i], s5_log_dt[i])

        q, k32, kb, v32, vb, u = _project(hp, pos_p, norm_mix[i], w_in_b, q_norm[i], k_norm[i], 512)
        att = _attn_prompt(q, kb, vb, lam_vec, subln[i], lam_init, 256)
        s5o, s_re, s_im = _s5_prompt(u, tabs, w_glu_b, b_glu[i], 256)
        h1, f = _outproj(hp, att, s5o, w_out_b, norm_ffn[i], 512)
        h2, cbuf = _ffn(f, h1, w_gate_b, w_up_b, w_down_b, conv_w[i], conv_b[i], None, 512, 512)
        hp = _ple(h2, p_prompt[i].reshape(s, -1), w_pg_b, w_pp_b, ple_norm[i], 512)
        outs[0].append(k32.reshape(b, s, N_HEADS, 2, HEAD_DIM))
        outs[1].append(v32.reshape(b, s, N_HEADS, V_DIM))
        outs[2].append(s_re[None])
        outs[3].append(s_im[None])
        outs[4].append(cbuf[None])

        q, k32, kb, v32, vb, u = _project(hs, pos_s, norm_mix[i], w_in_b, q_norm[i], k_norm[i], 512)
        att = _attn_sample(q, kb, vb, cache_k[i], cache_v[i], page_table, lam_vec, subln[i],
                           lam_init, 8)
        s5o, s_re, s_im = _s5_sample(u, state_s5_re[i], state_s5_im[i], tabs, w_glu_b, b_glu[i])
        h1, f = _outproj(hs, att, s5o, w_out_b, norm_ffn[i], 512)
        h2, cbuf = _ffn(f, h1, w_gate_b, w_up_b, w_down_b, conv_w[i], conv_b[i], state_conv[i],
                        512, 512)
        hs = _ple(h2, p_sample[i].reshape(db, -1), w_pg_b, w_pp_b, ple_norm[i], 512)
        outs[5].append(k32.reshape(db, t_new, N_HEADS, 2, HEAD_DIM))
        outs[6].append(v32.reshape(db, t_new, N_HEADS, V_DIM))
        outs[7].append(s_re)
        outs[8].append(s_im)
        outs[9].append(cbuf)
    return (hp.reshape(b, s, d), hs.reshape(db, t_new, d)) + tuple(jnp.stack(o) for o in outs)
```

```python
import functools
import math

import jax
import jax.numpy as jnp
from jax import lax
from jax.experimental import pallas as pl
from jax.experimental.pallas import tpu as pltpu

N_HEADS = 8
HEAD_DIM = 64
V_DIM = 2 * HEAD_DIM
ROT_DIM = HEAD_DIM // 4
ROPE_THETA = 500000.0
S5_CH = 16
S5_STATE = 64
CONV_W = 3
EPS = 1e-6
Q_SCALE = math.log2(math.e) * HEAD_DIM ** -0.5

V7X_LANES = 128
V7X_SUBLANES = 8
V7X_MXU_COLS = 256
VMEM_LIMIT_BYTES = 56 * 1024 * 1024

S5_SLAB_GROUPS = V7X_LANES // S5_CH
S5_SLAB_STATE = S5_SLAB_GROUPS * S5_STATE

F32 = jnp.float32
BF16 = jnp.bfloat16


def _tile(n, pref):
    t = min(n, pref)
    assert n % t == 0, (n, t)
    return t


def _params(sem, vmem=VMEM_LIMIT_BYTES):
    return pltpu.CompilerParams(dimension_semantics=sem, vmem_limit_bytes=vmem)


def _qk_post(p, gain, cos, sin_lo, sin_hi, seg):
    outs = []
    for c in range(p.shape[1] // V7X_LANES):
        pc = p[:, c * V7X_LANES:(c + 1) * V7X_LANES]
        ms = jnp.dot((pc * pc).astype(BF16), seg, preferred_element_type=F32)
        y = pc * lax.rsqrt(ms + EPS) * gain
        y = (y * cos
             + pltpu.roll(y, V7X_LANES - ROT_DIM // 2, 1) * sin_lo
             + pltpu.roll(y, ROT_DIM // 2, 1) * sin_hi)
        outs.append(y)
    return jnp.concatenate(outs, axis=1)


def _proj_kernel(x_ref, g_ref, w_ref, qg_ref, kg_ref, cos_ref, slo_ref, shi_ref, seg_ref,
                 q_ref, k32_ref, kb_ref, v32_ref, vb_ref, u_ref, xn_ref, *, v_transposed):
    j = pl.program_id(1)

    @pl.when(j == 0)
    def _():
        x = x_ref[...]
        ms = jnp.mean(x * x, axis=-1, keepdims=True)
        xn_ref[...] = (x * lax.rsqrt(ms + EPS) * g_ref[...]).astype(BF16)

    proj = jnp.dot(xn_ref[...], w_ref[...], preferred_element_type=F32)

    @pl.when(j == 0)
    def _():
        q = _qk_post(proj, qg_ref[...], cos_ref[...], slo_ref[...], shi_ref[...], seg_ref[...])
        q_ref[...] = (q * Q_SCALE).astype(BF16)

    @pl.when(j == 1)
    def _():
        k = _qk_post(proj, kg_ref[...], cos_ref[...], slo_ref[...], shi_ref[...], seg_ref[...])
        k32_ref[...] = k
        kb_ref[...] = k.astype(BF16)

    @pl.when(j == 2)
    def _():
        v32_ref[...] = proj
        if v_transposed:
            for h in range(N_HEADS):
                vb_ref[h, 0] = proj[:, h * V_DIM:(h + 1) * V_DIM].T.astype(BF16)
        else:
            vb_ref[...] = proj.astype(BF16)

    @pl.when(j == 3)
    def _():
        u_ref[...] = proj


def _rope_tables(pos):
    half = ROT_DIM // 2
    inv = ROPE_THETA ** (-(jnp.arange(half, dtype=F32) * 2.0 / ROT_DIM))
    ang = pos.astype(F32)[:, None] * inv[None, :]
    cos, sin = jnp.cos(ang), jnp.sin(ang)
    t = pos.shape[0]
    ones = jnp.ones((t, HEAD_DIM - ROT_DIM), F32)
    zeros = jnp.zeros((t, HEAD_DIM - ROT_DIM), F32)
    zh = jnp.zeros((t, half), F32)
    cos_t = jnp.concatenate([cos, cos, ones], axis=1)
    lo_t = jnp.concatenate([-sin, zh, zeros], axis=1)
    hi_t = jnp.concatenate([zh, sin, zeros], axis=1)
    rep = V7X_LANES // HEAD_DIM
    return (jnp.tile(cos_t, (1, rep)), jnp.tile(lo_t, (1, rep)), jnp.tile(hi_t, (1, rep)))


def _project(x, pos, norm_mix, w_in_b, q_norm, k_norm, tm, v_transposed):
    m, d = x.shape
    width = w_in_b.shape[1] // 4
    tm = _tile(m, tm)
    if v_transposed:
        vb_spec = pl.BlockSpec((N_HEADS, 1, V_DIM, tm), lambda i, j: (0, i, 0, 0))
        vb_shape = jax.ShapeDtypeStruct((N_HEADS, m // tm, V_DIM, tm), BF16)
    else:
        vb_spec = pl.BlockSpec((tm, width), lambda i, j: (i, 0))
        vb_shape = jax.ShapeDtypeStruct((m, width), BF16)
    cos_t, lo_t, hi_t = _rope_tables(pos)
    rep = V7X_LANES // HEAD_DIM
    qg = jnp.tile(q_norm.reshape(1, HEAD_DIM), (1, rep))
    kg = jnp.tile(k_norm.reshape(1, HEAD_DIM), (1, rep))
    lane = jnp.arange(V7X_LANES)
    seg = ((lane[:, None] // HEAD_DIM) == (lane[None, :] // HEAD_DIM)).astype(BF16) / HEAD_DIM
    row = lambda i, j: (i, 0)
    const = lambda i, j: (0, 0)
    out_spec = pl.BlockSpec((tm, width), row)
    return pl.pallas_call(
        functools.partial(_proj_kernel, v_transposed=v_transposed),
        grid=(m // tm, 4),
        in_specs=[
            pl.BlockSpec((tm, d), row),
            pl.BlockSpec((1, d), const),
            pl.BlockSpec((d, width), lambda i, j: (0, j)),
            pl.BlockSpec((1, V7X_LANES), const),
            pl.BlockSpec((1, V7X_LANES), const),
            pl.BlockSpec((tm, V7X_LANES), row),
            pl.BlockSpec((tm, V7X_LANES), row),
            pl.BlockSpec((tm, V7X_LANES), row),
            pl.BlockSpec((V7X_LANES, V7X_LANES), const),
        ],
        out_specs=[out_spec] * 4 + [vb_spec, out_spec],
        out_shape=[
            jax.ShapeDtypeStruct((m, width), BF16),
            jax.ShapeDtypeStruct((m, width), F32),
            jax.ShapeDtypeStruct((m, width), BF16),
            jax.ShapeDtypeStruct((m, width), F32),
            vb_shape,
            jax.ShapeDtypeStruct((m, width), F32),
        ],
        scratch_shapes=[pltpu.VMEM((tm, d), BF16)],
        compiler_params=_params(("parallel", "arbitrary")),
        name="in_proj",
    )(x, norm_mix.reshape(1, d), w_in_b, qg, kg, cos_t, lo_t, hi_t, seg)


def _lambda(lam_ref, lam_init):
    lv = lam_ref[...]
    l1 = jnp.sum(lv[0:1] * lv[1:2], axis=-1, keepdims=True)
    l2 = jnp.sum(lv[2:3] * lv[3:4], axis=-1, keepdims=True)
    return jnp.exp(l1) - jnp.exp(l2) + lam_init


def _subln(out, sub, lam_init):
    ms = jnp.mean(out * out, axis=-1, keepdims=True)
    return out * lax.rsqrt(ms + EPS) * sub * (1.0 - lam_init)


def _attn_kernel(q_ref, k_ref, vt_ref, lam_ref, sub_ref, o_ref, acc_ref, st_ref,
                 *, tq, tk, lam_init):
    qi = pl.program_id(1)
    tv = vt_ref.shape[-1]
    n_sub = tk // tv
    q = q_ref[...].astype(F32)
    lane = lax.broadcasted_iota(jnp.int32, q.shape, 1)
    qs = jnp.concatenate([jnp.where(lane < HEAD_DIM, q, 0.0),
                          jnp.where(lane >= HEAD_DIM, q, 0.0)], axis=0).astype(BF16)
    acc_ref[...] = jnp.zeros(acc_ref.shape, F32)

    def scores(kc, j):
        k0 = pl.multiple_of(kc * tk + j * tv, tv)
        st_ref[j] = lax.dot_general(k_ref[pl.ds(k0, tv), :], qs, (((1,), (1,)), ((), ())),
                                    preferred_element_type=F32)

    def absorb(kc, j, m, l, masked):
        st = st_ref[j]
        if masked:
            key = lax.broadcasted_iota(jnp.int32, st.shape, 0) + (kc * tk + j * tv - qi * tq)
            qry = lax.broadcasted_iota(jnp.int32, st.shape, 1)
            qry = jnp.where(qry >= tq, qry - tq, qry)
            st = jnp.where(key <= qry, st, -jnp.inf)
        m_new = jnp.maximum(m, jnp.max(st, axis=0, keepdims=True))
        alpha = jnp.exp2(m - m_new)
        p = jnp.exp2(st - m_new)
        l = alpha * l + jnp.sum(p, axis=0, keepdims=True)
        pv = jnp.dot(vt_ref[kc * n_sub + j], p.astype(BF16), preferred_element_type=F32)
        acc_ref[...] = alpha * acc_ref[...] + pv
        return m_new, l

    def chunk(kc, m, l, masked):
        for j in range(n_sub):
            if j + 1 < n_sub:
                scores(kc, j + 1)
            elif not masked:
                scores(kc + 1, 0)
            m, l = absorb(kc, j, m, l, masked)
        return m, l

    n_full = (qi * tq) // tk
    scores(0, 0)
    init = (jnp.full((1, 2 * tq), -jnp.inf, F32), jnp.zeros((1, 2 * tq), F32))
    m, l = lax.fori_loop(0, n_full, lambda kc, c: chunk(kc, c[0], c[1], False), init)
    m, l = chunk(n_full, m, l, True)

    lam = _lambda(lam_ref, lam_init)
    o = acc_ref[...] / l
    out = o[:, :tq] - lam * o[:, tq:]
    ms = jnp.mean(out * out, axis=0, keepdims=True)
    y = out * lax.rsqrt(ms + EPS) * sub_ref[...] * (1.0 - lam_init)
    o_ref[...] = y.T.astype(o_ref.dtype)


def _attn_prompt(q, k, vt, lam_vec, subln, lam_init, tq, tk):
    s, width = q.shape
    tv = vt.shape[-1]
    tq = _tile(s, tq)
    tk = _tile(s, max(tk, tv))
    assert tk % tq == 0 and tk % tv == 0
    kern = functools.partial(_attn_kernel, tq=tq, tk=tk, lam_init=lam_init)
    return pl.pallas_call(
        kern,
        grid=(N_HEADS, s // tq),
        in_specs=[
            pl.BlockSpec((tq, V_DIM), lambda h, i: (i, h)),
            pl.BlockSpec((s, V_DIM), lambda h, i: (0, h)),
            pl.BlockSpec((None, s // tv, V_DIM, tv), lambda h, i: (h, 0, 0, 0)),
            pl.BlockSpec((4, HEAD_DIM), lambda h, i: (0, 0)),
            pl.BlockSpec((V_DIM, 1), lambda h, i: (0, 0)),
        ],
        out_specs=pl.BlockSpec((tq, V_DIM), lambda h, i: (i, h)),
        out_shape=jax.ShapeDtypeStruct((s, width), BF16),
        scratch_shapes=[pltpu.VMEM((V_DIM, 2 * tq), F32),
                        pltpu.VMEM((tk // tv, tv, 2 * tq), F32)],
        compiler_params=_params(("parallel", "arbitrary")),
        name="attn_prompt",
    )(q, k, vt, lam_vec, subln.reshape(V_DIM, 1))


def _attn_sample_kernel(pt_ref, *refs, n_pg, lam_init):
    k_refs = refs[:n_pg]
    v_refs = refs[n_pg:2 * n_pg]
    (q_ref, kn_ref, vn_ref, exp_ref, lam_ref, sub_ref, o_ref,
     qm_ref, m_ref, l_ref, acc_ref) = refs[2 * n_pg:]
    s_id = pl.program_id(1)
    n_rows = 2 * N_HEADS
    width = q_ref.shape[-1]
    page = k_refs[0].shape[1]

    @pl.when(s_id == 0)
    def _():
        q = jnp.broadcast_to(q_ref[0].astype(F32), (n_rows, width))
        r = lax.broadcasted_iota(jnp.int32, (n_rows, width), 0)
        c = lax.broadcasted_iota(jnp.int32, (n_rows, width), 1)
        keep = (c >= r * HEAD_DIM) & (c < (r + 1) * HEAD_DIM)
        qm_ref[...] = jnp.where(keep, q, 0.0).astype(BF16)
        m_ref[...] = jnp.full(m_ref.shape, -jnp.inf, F32)
        l_ref[...] = jnp.zeros(l_ref.shape, F32)
        acc_ref[...] = jnp.zeros(acc_ref.shape, F32)

    qm = qm_ref[...]
    s = jnp.concatenate(
        [jnp.dot(qm, k_refs[g][...].astype(BF16), preferred_element_type=F32)
         for g in range(n_pg)], axis=1)
    m_prev = m_ref[...]
    m_new = jnp.maximum(m_prev, jnp.max(s, axis=1, keepdims=True))
    alpha = jnp.exp2(m_prev - m_new)
    p = jnp.exp2(s - m_new)
    l_ref[...] = alpha * l_ref[...] + jnp.sum(p, axis=1, keepdims=True)
    m_ref[...] = m_new
    pb = p.astype(BF16)
    p_rows = jnp.concatenate([pb[:, g * page:(g + 1) * page] for g in range(n_pg)], axis=0)
    p_wide = jnp.dot(p_rows, exp_ref[...], preferred_element_type=F32)
    r = lax.broadcasted_iota(jnp.int32, p_wide.shape, 0)
    c = lax.broadcasted_iota(jnp.int32, p_wide.shape, 1)
    own = (c % N_HEADS) == ((r % n_rows) // 2)
    p_wide = jnp.where(own, p_wide, 0.0).astype(BF16)
    acc = alpha * acc_ref[...]
    for g in range(n_pg):
        acc = acc + jnp.dot(p_wide[g * n_rows:(g + 1) * n_rows], v_refs[g][...].astype(BF16),
                            preferred_element_type=F32)
    acc_ref[...] = acc

    @pl.when(s_id == pl.num_programs(1) - 1)
    def _():
        kn = kn_ref[0].astype(F32)
        vn = vn_ref[0].astype(F32)
        s_new = jnp.sum(qm.astype(F32) * kn, axis=1, keepdims=True)
        m_old = m_ref[...]
        m_fin = jnp.maximum(m_old, s_new)
        a = jnp.exp2(m_old - m_fin)
        p_new = jnp.exp2(s_new - m_fin)
        l_fin = a * l_ref[...] + p_new
        acc_fin = a * acc_ref[...] + p_new.astype(BF16).astype(F32) * vn
        o = acc_fin / l_fin
        o1 = jnp.concatenate([o[2 * h:2 * h + 1] for h in range(N_HEADS)], axis=0)
        o2 = jnp.concatenate([o[2 * h + 1:2 * h + 2] for h in range(N_HEADS)], axis=0)
        lam = _lambda(lam_ref, lam_init)
        o_ref[0] = _subln(o1 - lam * o2, sub_ref[...], lam_init)


def _attn_sample(q, k_new, v_new, ck, cv, page_idx, lam_vec, subln, lam_init, n_pg):
    db, width = q.shape
    page = ck.shape[2]
    n_pages = page_idx.shape[1]
    n_pg = _tile(n_pages, n_pg)
    n_rows = 2 * N_HEADS
    tok = jnp.arange(page)
    expand = (tok[:, None] == (jnp.arange(page * N_HEADS)[None, :] // N_HEADS)).astype(BF16)
    vn = jnp.repeat(v_new.reshape(db, N_HEADS, V_DIM), 2, axis=1)

    def page_spec(shape, g):
        return pl.BlockSpec((None,) + shape, lambda b, s, pt, g=g: (pt[b, s * n_pg + g], 0, 0))

    const = lambda b, s, pt: (0, 0)
    seq = lambda b, s, pt: (b, 0, 0)
    kern = functools.partial(_attn_sample_kernel, n_pg=n_pg, lam_init=lam_init)
    out = pl.pallas_call(
        kern,
        grid_spec=pltpu.PrefetchScalarGridSpec(
            num_scalar_prefetch=1,
            grid=(db, n_pages // n_pg),
            in_specs=([page_spec((width, page), g) for g in range(n_pg)]
                      + [page_spec((page * N_HEADS, V_DIM), g) for g in range(n_pg)]
                      + [pl.BlockSpec((1, 1, width), seq), pl.BlockSpec((1, 1, width), seq),
                         pl.BlockSpec((1, n_rows, V_DIM), seq),
                         pl.BlockSpec(expand.shape, const),
                         pl.BlockSpec((4, HEAD_DIM), const),
                         pl.BlockSpec((1, V_DIM), const)]),
            out_specs=pl.BlockSpec((1, N_HEADS, V_DIM), seq),
            scratch_shapes=[pltpu.VMEM((n_rows, width), BF16),
                            pltpu.VMEM((n_rows, 1), F32), pltpu.VMEM((n_rows, 1), F32),
                            pltpu.VMEM((n_rows, V_DIM), F32)],
        ),
        out_shape=jax.ShapeDtypeStruct((db, N_HEADS, V_DIM), F32),
        compiler_params=_params(("parallel", "arbitrary")),
        name="attn_sample",
    )(page_idx, *([ck] * n_pg), *([cv] * n_pg),
      q.reshape(db, 1, width), k_new.reshape(db, 1, width), vn, expand,
      lam_vec, subln.reshape(1, V_DIM))
    return out.reshape(db, width).astype(BF16)


def _s5_tables(a_re, a_im, b_re, b_im, c_re, c_im, d, log_dt):
    g, p = a_re.shape
    n_slab = g // S5_SLAB_GROUPS
    dt = jnp.exp(log_dt.astype(F32))[:, None]
    mag = jnp.exp(dt * a_re)
    ab_re = mag * jnp.cos(dt * a_im)
    ab_im = mag * jnp.sin(dt * a_im)
    den = a_re * a_re + a_im * a_im
    n_re = ab_re - 1.0
    f_re = (n_re * a_re + ab_im * a_im) / den
    f_im = (ab_im * a_re - n_re * a_im) / den
    bb_re = f_re[..., None] * b_re - f_im[..., None] * b_im
    bb_im = f_re[..., None] * b_im + f_im[..., None] * b_re
    eye = jnp.eye(S5_SLAB_GROUPS, dtype=F32)

    def in_mat(bb):
        bb = bb.reshape(n_slab, S5_SLAB_GROUPS, p, S5_CH)
        return jnp.einsum('jgpc,gh->jgchp', bb, eye).reshape(n_slab, V7X_LANES, S5_SLAB_STATE)

    def out_mat(cc):
        cc = cc.reshape(n_slab, S5_SLAB_GROUPS, S5_CH, p)
        return jnp.einsum('jgcp,gh->jgphc', cc, eye).reshape(n_slab, S5_SLAB_STATE, V7X_LANES)

    w_in = jnp.concatenate([in_mat(bb_re), in_mat(bb_im)], axis=2).astype(BF16)
    w_out = jnp.concatenate([out_mat(c_re.astype(F32)), out_mat(-c_im.astype(F32))],
                            axis=1).astype(BF16)

    pw_re, pw_im = [ab_re], [ab_im]
    for _ in range(V7X_SUBLANES - 1):
        r, i = pw_re[-1], pw_im[-1]
        pw_re.append(r * ab_re - i * ab_im)
        pw_im.append(r * ab_im + i * ab_re)
    flat = lambda t: t.reshape(n_slab, 1, S5_SLAB_STATE)
    p_re = jnp.concatenate([flat(t) for t in pw_re], axis=1)
    p_im = jnp.concatenate([flat(t) for t in pw_im], axis=1)
    rows = jnp.arange(V7X_SUBLANES)[None, :, None]
    steps = []
    for k in (1, 2, 4):
        steps.append(jnp.where(rows >= k, flat(pw_re[k - 1]), 0.0))
        steps.append(jnp.where(rows >= k, flat(pw_im[k - 1]), 0.0))
    scan = jnp.stack([p_re, p_im] + steps, axis=1)
    dvec = d.astype(F32).reshape(n_slab, 1, V7X_LANES)
    return w_in, w_out, scan, dvec


def _glu(y, wglu_ref, bglu_ref):
    g = jax.nn.gelu(y)
    z = jnp.dot(g.astype(BF16), wglu_ref[...], preferred_element_type=F32) + bglu_ref[...]
    return g * jax.nn.sigmoid(z)


def _s5_prompt_kernel(u_ref, win_ref, wout_ref, scan_ref, d_ref, wglu_ref, bglu_ref,
                      o_ref, st_ref, x_scr, y_scr, c_scr, *, tt):
    t = pl.program_id(0)
    n_slab = win_ref.shape[0]
    ns = S5_SLAB_STATE

    @pl.when(t == 0)
    def _():
        c_scr[...] = jnp.zeros(c_scr.shape, F32)

    for j in range(n_slab):
        lanes = slice(j * V7X_LANES, (j + 1) * V7X_LANES)
        u_j = u_ref[:, lanes]
        x_scr[...] = jnp.dot(u_j.astype(BF16), win_ref[j], preferred_element_type=F32)
        p_re, p_im = scan_ref[j, 0], scan_ref[j, 1]

        def body(i, carry):
            cr, ci = carry
            r0 = pl.multiple_of(i * V7X_SUBLANES, V7X_SUBLANES)
            xr = x_scr[pl.ds(r0, V7X_SUBLANES), 0:ns]
            xi = x_scr[pl.ds(r0, V7X_SUBLANES), ns:2 * ns]
            for n, k in enumerate((1, 2, 4)):
                mr, mi = scan_ref[j, 2 + 2 * n], scan_ref[j, 3 + 2 * n]
                sr = pltpu.roll(xr, k, 0)
                si = pltpu.roll(xi, k, 0)
                xr, xi = xr + mr * sr - mi * si, xi + mr * si + mi * sr
            xr, xi = xr + p_re * cr - p_im * ci, xi + p_re * ci + p_im * cr
            x_scr[pl.ds(r0, V7X_SUBLANES), 0:ns] = xr
            x_scr[pl.ds(r0, V7X_SUBLANES), ns:2 * ns] = xi
            return (xr[V7X_SUBLANES - 1:V7X_SUBLANES], xi[V7X_SUBLANES - 1:V7X_SUBLANES])

        cr, ci = lax.fori_loop(0, tt // V7X_SUBLANES, body,
                               (c_scr[j:j + 1, 0:ns], c_scr[j:j + 1, ns:2 * ns]))
        c_scr[j:j + 1, 0:ns] = cr
        c_scr[j:j + 1, ns:2 * ns] = ci
        y = jnp.dot(x_scr[...].astype(BF16), wout_ref[j], preferred_element_type=F32)
        y_scr[:, lanes] = y + d_ref[j] * u_j

    o_ref[...] = _glu(y_scr[...], wglu_ref, bglu_ref).astype(o_ref.dtype)
    st_ref[...] = c_scr[...]


def _s5_prompt(u, tabs, w_glu_b, b_glu, tt):
    s, width = u.shape
    w_in, w_out, scan, dvec = tabs
    n_slab = w_in.shape[0]
    tt = _tile(s, tt)
    full = lambda a: pl.BlockSpec(a.shape, lambda t: (0,) * a.ndim)
    bglu = b_glu.reshape(1, width).astype(F32)
    kern = functools.partial(_s5_prompt_kernel, tt=tt)
    out, st = pl.pallas_call(
        kern,
        grid=(s // tt,),
        in_specs=[pl.BlockSpec((tt, width), lambda t: (t, 0)),
                  full(w_in), full(w_out), full(scan), full(dvec), full(w_glu_b), full(bglu)],
        out_specs=[pl.BlockSpec((tt, width), lambda t: (t, 0)),
                   pl.BlockSpec((n_slab, 2 * S5_SLAB_STATE), lambda t: (0, 0))],
        out_shape=[jax.ShapeDtypeStruct((s, width), BF16),
                   jax.ShapeDtypeStruct((n_slab, 2 * S5_SLAB_STATE), F32)],
        scratch_shapes=[pltpu.VMEM((tt, 2 * S5_SLAB_STATE), F32),
                        pltpu.VMEM((tt, width), F32),
                        pltpu.VMEM((n_slab, 2 * S5_SLAB_STATE), F32)],
        compiler_params=_params(("arbitrary",)),
        name="s5_prompt",
    )(u, w_in, w_out, scan, dvec, w_glu_b, bglu)
    groups = n_slab * S5_SLAB_GROUPS
    s_re = st[:, :S5_SLAB_STATE].reshape(groups, S5_STATE)
    s_im = st[:, S5_SLAB_STATE:].reshape(groups, S5_STATE)
    return out, s_re, s_im


def _s5_sample_kernel(u_ref, sre_ref, sim_ref, win_ref, wout_ref, scan_ref, d_ref, wglu_ref,
                      bglu_ref, o_ref, xre_ref, xim_ref, y_scr):
    n_slab = win_ref.shape[0]
    ns = S5_SLAB_STATE
    for j in range(n_slab):
        lanes = slice(j * V7X_LANES, (j + 1) * V7X_LANES)
        st = slice(j * ns, (j + 1) * ns)
        u_j = u_ref[:, lanes]
        bu = jnp.dot(u_j.astype(BF16), win_ref[j], preferred_element_type=F32)
        ab_re, ab_im = scan_ref[j, 0, 0:1], scan_ref[j, 1, 0:1]
        s_re, s_im = sre_ref[:, st], sim_ref[:, st]
        xr = ab_re * s_re - ab_im * s_im + bu[:, 0:ns]
        xi = ab_re * s_im + ab_im * s_re + bu[:, ns:2 * ns]
        xre_ref[:, st] = xr
        xim_ref[:, st] = xi
        x = jnp.concatenate([xr, xi], axis=1).astype(BF16)
        y = jnp.dot(x, wout_ref[j], preferred_element_type=F32)
        y_scr[:, lanes] = y + d_ref[j] * u_j
    o_ref[...] = _glu(y_scr[...], wglu_ref, bglu_ref).astype(o_ref.dtype)


def _s5_sample(u, s_re, s_im, tabs, w_glu_b, b_glu):
    db, width = u.shape
    w_in, w_out, scan, dvec = tabs
    n_state = s_re.shape[1] * s_re.shape[2]
    bglu = b_glu.reshape(1, width).astype(F32)
    args = (u, s_re.reshape(db, n_state).astype(F32), s_im.reshape(db, n_state).astype(F32),
            w_in, w_out, scan, dvec, w_glu_b, bglu)
    full = lambda a: pl.BlockSpec(a.shape, lambda i: (0,) * a.ndim)
    out, x_re, x_im = pl.pallas_call(
        _s5_sample_kernel,
        grid=(1,),
        in_specs=[full(a) for a in args],
        out_specs=[pl.BlockSpec((db, width), lambda i: (0, 0)),
                   pl.BlockSpec((db, n_state), lambda i: (0, 0)),
                   pl.BlockSpec((db, n_state), lambda i: (0, 0))],
        out_shape=[jax.ShapeDtypeStruct((db, width), BF16),
                   jax.ShapeDtypeStruct((db, n_state), F32),
                   jax.ShapeDtypeStruct((db, n_state), F32)],
        scratch_shapes=[pltpu.VMEM((db, width), F32)],
        compiler_params=_params(("arbitrary",)),
        name="s5_sample",
    )(*args)
    return out, x_re.reshape(s_re.shape), x_im.reshape(s_im.shape)


def _outproj_kernel(x_ref, a_ref, s_ref, w_ref, g_ref, h_ref, f_ref):
    half = a_ref.shape[1]
    h = (x_ref[...]
         + jnp.dot(a_ref[...], w_ref[0:half, :], preferred_element_type=F32)
         + jnp.dot(s_ref[...], w_ref[half:, :], preferred_element_type=F32))
    h_ref[...] = h
    ms = jnp.mean(h * h, axis=-1, keepdims=True)
    f_ref[...] = (h * lax.rsqrt(ms + EPS) * g_ref[...]).astype(BF16)


def _outproj(x, att, s5o, w_out_b, norm_ffn, tm):
    m, d = x.shape
    half = att.shape[1]
    tm = _tile(m, tm)
    row = lambda i: (i, 0)
    const = lambda i: (0, 0)
    return pl.pallas_call(
        _outproj_kernel,
        grid=(m // tm,),
        in_specs=[pl.BlockSpec((tm, d), row), pl.BlockSpec((tm, half), row),
                  pl.BlockSpec((tm, half), row), pl.BlockSpec((d, d), const),
                  pl.BlockSpec((1, d), const)],
        out_specs=[pl.BlockSpec((tm, d), row), pl.BlockSpec((tm, d), row)],
        out_shape=[jax.ShapeDtypeStruct((m, d), F32), jax.ShapeDtypeStruct((m, d), BF16)],
        compiler_params=_params(("parallel",)),
        name="out_proj",
    )(x, att, s5o, w_out_b, norm_ffn.reshape(1, d))


def _ffn_down(acts, wd_ref, h_ref, o_ref):
    @pl.when(pl.program_id(1) == 0)
    def _():
        o_ref[...] = h_ref[...]

    act = acts[0] if len(acts) == 1 else jnp.concatenate(acts, axis=1)
    o_ref[...] += jnp.dot(act, wd_ref[...], preferred_element_type=F32)


def _ffn_prompt_kernel(f_ref, h_ref, wg_ref, wu_ref, wd_ref, cw_ref, cb_ref,
                       o_ref, buf_ref, hgx_ref, prev_ref, *, tm, tc):
    i = pl.program_id(0)
    j = pl.program_id(1)
    halo = V7X_SUBLANES
    tf = wg_ref.shape[1]
    f = f_ref[...]

    @pl.when(i == 0)
    def _():
        hgx_ref[0:halo, :] = jnp.zeros((halo, tf), F32)

    @pl.when(i > 0)
    def _():
        hgx_ref[0:halo, :] = prev_ref[j]

    acts = []
    for c0 in range(0, tf, tc):
        cs = slice(c0, c0 + tc)
        hg = jnp.dot(f, wg_ref[:, cs], preferred_element_type=F32)
        hu = jnp.dot(f, wu_ref[:, cs], preferred_element_type=F32)
        hgx_ref[halo:, cs] = hg
        cw = cw_ref[:, cs]
        conv = cb_ref[:, cs] + cw[2:3] * hg
        for tap in range(CONV_W - 1):
            off = halo - (CONV_W - 1) + tap
            conv = conv + cw[tap:tap + 1] * hgx_ref[pl.ds(off, tm), cs]
        acts.append((jax.nn.gelu(conv) * hu).astype(BF16))
    tail = hgx_ref[tm:tm + halo, :]
    prev_ref[j] = tail
    buf_ref[...] = tail
    _ffn_down(acts, wd_ref, h_ref, o_ref)


def _ffn_sample_kernel(f_ref, h_ref, wg_ref, wu_ref, wd_ref, cw_ref, cb_ref, b0_ref, b1_ref,
                       o_ref, hg_ref):
    f = f_ref[...]
    hg = jnp.dot(f, wg_ref[...], preferred_element_type=F32)
    hu = jnp.dot(f, wu_ref[...], preferred_element_type=F32)
    hg_ref[...] = hg
    cw = cw_ref[...]
    conv = cb_ref[...] + cw[0:1] * b0_ref[...] + cw[1:2] * b1_ref[...] + cw[2:3] * hg
    _ffn_down([(jax.nn.gelu(conv) * hu).astype(BF16)], wd_ref, h_ref, o_ref)


def _ffn(f, h, w_gate_b, w_up_b, w_down_b, conv_w, conv_b, conv_buf, tm, tf):
    m, d = f.shape
    dff = w_gate_b.shape[1]
    tm = _tile(m, tm)
    tf = _tile(dff, tf)
    n_ff = dff // tf
    row = lambda i, j: (i, 0)
    col = lambda i, j: (0, j)
    common_in = [pl.BlockSpec((tm, d), row), pl.BlockSpec((tm, d), row),
                 pl.BlockSpec((d, tf), col), pl.BlockSpec((d, tf), col),
                 pl.BlockSpec((tf, d), lambda i, j: (j, 0)),
                 pl.BlockSpec((CONV_W, tf), col), pl.BlockSpec((1, tf), col)]
    common_args = (f, h, w_gate_b, w_up_b, w_down_b, conv_w.astype(F32),
                   conv_b.reshape(1, dff).astype(F32))
    if conv_buf is None:
        halo = V7X_SUBLANES
        kern = functools.partial(_ffn_prompt_kernel, tm=tm, tc=_tile(tf, V7X_MXU_COLS))
        out, tail = pl.pallas_call(
            kern,
            grid=(m // tm, n_ff),
            in_specs=common_in,
            out_specs=[pl.BlockSpec((tm, d), row),
                       pl.BlockSpec((None, halo, tf), lambda i, j: (i, 0, j))],
            out_shape=[jax.ShapeDtypeStruct((m, d), F32),
                       jax.ShapeDtypeStruct((m // tm, halo, dff), F32)],
            scratch_shapes=[pltpu.VMEM((tm + halo, tf), F32),
                            pltpu.VMEM((n_ff, halo, tf), F32)],
            compiler_params=_params(("arbitrary", "arbitrary")),
            name="ffn_prompt",
        )(*common_args)
        return out, tail[-1, halo - (CONV_W - 1):]
    b0 = conv_buf[:, 0, :].astype(F32)
    b1 = conv_buf[:, 1, :].astype(F32)
    out, hg = pl.pallas_call(
        _ffn_sample_kernel,
        grid=(m // tm, n_ff),
        in_specs=common_in + [pl.BlockSpec((tm, tf), lambda i, j: (i, j))] * 2,
        out_specs=[pl.BlockSpec((tm, d), row), pl.BlockSpec((tm, tf), lambda i, j: (i, j))],
        out_shape=[jax.ShapeDtypeStruct((m, d), F32), jax.ShapeDtypeStruct((m, dff), F32)],
        compiler_params=_params(("parallel", "arbitrary")),
        name="ffn_sample",
    )(*common_args, b0, b1)
    return out, jnp.stack([b1, hg], axis=1)


def _ple_kernel(h_ref, p_ref, wg_ref, wp_ref, g_ref, o_ref):
    h = h_ref[...]
    gate = jax.nn.sigmoid(jnp.dot(h.astype(BF16), wg_ref[...], preferred_element_type=F32))
    e = jnp.dot(p_ref[...].astype(BF16), wp_ref[...], preferred_element_type=F32)
    ms = jnp.mean(e * e, axis=-1, keepdims=True)
    o_ref[...] = h + gate * (e * lax.rsqrt(ms + EPS) * g_ref[...])


def _ple(h, p, w_gate_b, w_proj_b, ple_norm, tm):
    m, d = h.shape
    pd = p.shape[1]
    tm = _tile(m, tm)
    row = lambda i: (i, 0)
    const = lambda i: (0, 0)
    return pl.pallas_call(
        _ple_kernel,
        grid=(m // tm,),
        in_specs=[pl.BlockSpec((tm, d), row), pl.BlockSpec((tm, pd), row),
                  pl.BlockSpec((d, d), const), pl.BlockSpec((pd, d), const),
                  pl.BlockSpec((1, d), const)],
        out_specs=pl.BlockSpec((tm, d), row),
        out_shape=jax.ShapeDtypeStruct((m, d), F32),
        compiler_params=_params(("parallel",)),
        name="ple_gate",
    )(h, p, w_gate_b, w_proj_b, ple_norm.reshape(1, d))


def kernel(x_prompt, x_sample, cache_k, cache_v, state_s5_re, state_s5_im, state_conv, page_table, p_prompt, p_sample, norm_mix, w_in, q_norm, k_norm, lam_q1, lam_k1, lam_q2, lam_k2, subln, s5_a_re, s5_a_im, s5_b_re, s5_b_im, s5_c_re, s5_c_im, s5_d, s5_log_dt, w_glu, b_glu, w_out, norm_ffn, w_gate, w_up, conv_w, conv_b, w_down, w_ple_gate, w_ple_proj, ple_norm):
    depth = w_in.shape[0]
    b, s, d = x_prompt.shape
    db, t_new, _ = x_sample.shape
    assert b == 1 and t_new == 1
    page = cache_k.shape[2]
    past_len = page_table.shape[1] * page
    hp = x_prompt.reshape(s, d)
    hs = x_sample.reshape(db, d)
    pos_p = jnp.arange(s)
    pos_s = jnp.full((db,), past_len)
    n_pool = cache_k.shape[1]
    ck = jnp.transpose(cache_k, (0, 1, 3, 4, 5, 2)).reshape(depth * n_pool, -1, page)
    cv = cache_v.reshape(depth * n_pool, page * N_HEADS, V_DIM)
    outs = [[] for _ in range(10)]
    for i in range(depth):
        lam_init = 0.8 - 0.6 * math.exp(-0.3 * i)
        lam_vec = jnp.stack([lam_q1[i], lam_k1[i], lam_q2[i], lam_k2[i]]).astype(F32)
        w_in_b = w_in[i].astype(BF16)
        w_glu_b = w_glu[i].astype(BF16)
        w_out_b = w_out[i].astype(BF16)
        w_gate_b = w_gate[i].astype(BF16)
        w_up_b = w_up[i].astype(BF16)
        w_down_b = w_down[i].astype(BF16)
        w_pg_b = w_ple_gate[i].astype(BF16)
        w_pp_b = w_ple_proj[i].astype(BF16)
        tabs = _s5_tables(s5_a_re[i], s5_a_im[i], s5_b_re[i], s5_b_im[i], s5_c_re[i],
                          s5_c_im[i], s5_d[i], s5_log_dt[i])

        q, k32, kb, v32, vt, u = _project(hp, pos_p, norm_mix[i], w_in_b, q_norm[i], k_norm[i],
                                          512, True)
        att = _attn_prompt(q, kb, vt, lam_vec, subln[i], lam_init, 256, 1024)
        s5o, s_re, s_im = _s5_prompt(u, tabs, w_glu_b, b_glu[i], 256)
        h1, f = _outproj(hp, att, s5o, w_out_b, norm_ffn[i], 512)
        h2, cbuf = _ffn(f, h1, w_gate_b, w_up_b, w_down_b, conv_w[i], conv_b[i], None, 512, 512)
        hp = _ple(h2, p_prompt[i].reshape(s, -1), w_pg_b, w_pp_b, ple_norm[i], 512)
        outs[0].append(k32.reshape(b, s, N_HEADS, 2, HEAD_DIM))
        outs[1].append(v32.reshape(b, s, N_HEADS, V_DIM))
        outs[2].append(s_re[None])
        outs[3].append(s_im[None])
        outs[4].append(cbuf[None])

        q, k32, kb, v32, vb, u = _project(hs, pos_s, norm_mix[i], w_in_b, q_norm[i], k_norm[i],
                                          512, False)
        att = _attn_sample(q, kb, vb, ck, cv, page_table + i * n_pool, lam_vec, subln[i],
                           lam_init, 8)
        s5o, s_re, s_im = _s5_sample(u, state_s5_re[i], state_s5_im[i], tabs, w_glu_b, b_glu[i])
        h1, f = _outproj(hs, att, s5o, w_out_b, norm_ffn[i], 512)
        h2, cbuf = _ffn(f, h1, w_gate_b, w_up_b, w_down_b, conv_w[i], conv_b[i], state_conv[i],
                        512, 512)
        hs = _ple(h2, p_sample[i].reshape(db, -1), w_pg_b, w_pp_b, ple_norm[i], 512)
        outs[5].append(k32.reshape(db, t_new, N_HEADS, 2, HEAD_DIM))
        outs[6].append(v32.reshape(db, t_new, N_HEADS, V_DIM))
        outs[7].append(s_re)
        outs[8].append(s_im)
        outs[9].append(cbuf)
    return (hp.reshape(b, s, d), hs.reshape(db, t_new, d)) + tuple(jnp.stack(o) for o in outs)
```

```python
import functools
import math

import jax
import jax.numpy as jnp
from jax import lax
from jax.experimental import pallas as pl
from jax.experimental.pallas import tpu as pltpu

N_HEADS = 8
HEAD_DIM = 64
V_DIM = 2 * HEAD_DIM
ROT_DIM = HEAD_DIM // 4
ROPE_THETA = 500000.0
S5_CH = 16
S5_STATE = 64
CONV_W = 3
EPS = 1e-6
Q_SCALE = math.log2(math.e) * HEAD_DIM ** -0.5

V7X_LANES = 128
V7X_SUBLANES = 8
V7X_MXU_COLS = 256
VMEM_LIMIT_BYTES = 56 * 1024 * 1024

S5_SLAB_GROUPS = V7X_LANES // S5_CH
S5_SLAB_STATE = S5_SLAB_GROUPS * S5_STATE

F32 = jnp.float32
BF16 = jnp.bfloat16


def _tile(n, pref):
    t = min(n, pref)
    assert n % t == 0, (n, t)
    return t


def _params(sem, vmem=VMEM_LIMIT_BYTES):
    return pltpu.CompilerParams(dimension_semantics=sem, vmem_limit_bytes=vmem)


def _qk_post(p, gain, cos, sin_lo, sin_hi, seg):
    outs = []
    for c in range(p.shape[1] // V7X_LANES):
        pc = p[:, c * V7X_LANES:(c + 1) * V7X_LANES]
        ms = jnp.dot((pc * pc).astype(BF16), seg, preferred_element_type=F32)
        y = pc * lax.rsqrt(ms + EPS) * gain
        y = (y * cos
             + pltpu.roll(y, V7X_LANES - ROT_DIM // 2, 1) * sin_lo
             + pltpu.roll(y, ROT_DIM // 2, 1) * sin_hi)
        outs.append(y)
    return jnp.concatenate(outs, axis=1)


def _proj_kernel(x_ref, g_ref, w_ref, qg_ref, kg_ref, cos_ref, slo_ref, shi_ref, seg_ref,
                 q_ref, k32_ref, kb_ref, v32_ref, vb_ref, u_ref, *, v_transposed, bw):
    width = q_ref.shape[1]
    nb = width // bw
    x = x_ref[...]
    ms = jnp.mean(x * x, axis=-1, keepdims=True)
    xn = (x * lax.rsqrt(ms + EPS) * g_ref[...]).astype(BF16)
    rope = (cos_ref[...], slo_ref[...], shi_ref[...], seg_ref[...])

    def block(c):
        return jnp.dot(xn, w_ref[:, c * bw:(c + 1) * bw], preferred_element_type=F32)

    for b in range(nb):
        cols = slice(b * bw, (b + 1) * bw)
        q = _qk_post(block(b), qg_ref[...], *rope)
        q_ref[:, cols] = (q * Q_SCALE).astype(BF16)
    for b in range(nb):
        cols = slice(b * bw, (b + 1) * bw)
        k = _qk_post(block(nb + b), kg_ref[...], *rope)
        k32_ref[:, cols] = k
        kb_ref[:, cols] = k.astype(BF16)
    for b in range(nb):
        cols = slice(b * bw, (b + 1) * bw)
        v = block(2 * nb + b)
        v32_ref[:, cols] = v
        if v_transposed:
            for hh in range(bw // V_DIM):
                vb_ref[b * (bw // V_DIM) + hh, 0] = (
                    v[:, hh * V_DIM:(hh + 1) * V_DIM].T.astype(BF16))
        else:
            vb_ref[:, cols] = v.astype(BF16)
    for b in range(nb):
        u_ref[:, b * bw:(b + 1) * bw] = block(3 * nb + b)


def _rope_tables(pos):
    half = ROT_DIM // 2
    inv = ROPE_THETA ** (-(jnp.arange(half, dtype=F32) * 2.0 / ROT_DIM))
    ang = pos.astype(F32)[:, None] * inv[None, :]
    cos, sin = jnp.cos(ang), jnp.sin(ang)
    t = pos.shape[0]
    ones = jnp.ones((t, HEAD_DIM - ROT_DIM), F32)
    zeros = jnp.zeros((t, HEAD_DIM - ROT_DIM), F32)
    zh = jnp.zeros((t, half), F32)
    cos_t = jnp.concatenate([cos, cos, ones], axis=1)
    lo_t = jnp.concatenate([-sin, zh, zeros], axis=1)
    hi_t = jnp.concatenate([zh, sin, zeros], axis=1)
    rep = V7X_LANES // HEAD_DIM
    return (jnp.tile(cos_t, (1, rep)), jnp.tile(lo_t, (1, rep)), jnp.tile(hi_t, (1, rep)))


def _project(x, pos, norm_mix, w_in_b, q_norm, k_norm, tm, v_transposed):
    m, d = x.shape
    width = w_in_b.shape[1] // 4
    tm = _tile(m, tm)
    if v_transposed:
        vb_spec = pl.BlockSpec((N_HEADS, 1, V_DIM, tm), lambda i: (0, i, 0, 0))
        vb_shape = jax.ShapeDtypeStruct((N_HEADS, m // tm, V_DIM, tm), BF16)
    else:
        vb_spec = pl.BlockSpec((tm, width), lambda i: (i, 0))
        vb_shape = jax.ShapeDtypeStruct((m, width), BF16)
    cos_t, lo_t, hi_t = _rope_tables(pos)
    rep = V7X_LANES // HEAD_DIM
    qg = jnp.tile(q_norm.reshape(1, HEAD_DIM), (1, rep))
    kg = jnp.tile(k_norm.reshape(1, HEAD_DIM), (1, rep))
    lane = jnp.arange(V7X_LANES)
    seg = ((lane[:, None] // HEAD_DIM) == (lane[None, :] // HEAD_DIM)).astype(BF16) / HEAD_DIM
    row = lambda i: (i, 0)
    const = lambda i: (0, 0)
    out_spec = pl.BlockSpec((tm, width), row)
    kern = functools.partial(_proj_kernel, v_transposed=v_transposed,
                             bw=_tile(width, 2 * V7X_MXU_COLS))
    return pl.pallas_call(
        kern,
        grid=(m // tm,),
        in_specs=[
            pl.BlockSpec((tm, d), row),
            pl.BlockSpec((1, d), const),
            pl.BlockSpec((d, 4 * width), const, pipeline_mode=pl.Buffered(1)),
            pl.BlockSpec((1, V7X_LANES), const),
            pl.BlockSpec((1, V7X_LANES), const),
            pl.BlockSpec((tm, V7X_LANES), row),
            pl.BlockSpec((tm, V7X_LANES), row),
            pl.BlockSpec((tm, V7X_LANES), row),
            pl.BlockSpec((V7X_LANES, V7X_LANES), const),
        ],
        out_specs=[out_spec] * 4 + [vb_spec, out_spec],
        out_shape=[
            jax.ShapeDtypeStruct((m, width), BF16),
            jax.ShapeDtypeStruct((m, width), F32),
            jax.ShapeDtypeStruct((m, width), BF16),
            jax.ShapeDtypeStruct((m, width), F32),
            vb_shape,
            jax.ShapeDtypeStruct((m, width), F32),
        ],
        compiler_params=_params(("parallel",)),
        name="in_proj",
    )(x, norm_mix.reshape(1, d), w_in_b, qg, kg, cos_t, lo_t, hi_t, seg)


def _lambda(lam_ref, lam_init):
    lv = lam_ref[...]
    l1 = jnp.sum(lv[0:1] * lv[1:2], axis=-1, keepdims=True)
    l2 = jnp.sum(lv[2:3] * lv[3:4], axis=-1, keepdims=True)
    return jnp.exp(l1) - jnp.exp(l2) + lam_init


def _subln(out, sub, lam_init):
    ms = jnp.mean(out * out, axis=-1, keepdims=True)
    return out * lax.rsqrt(ms + EPS) * sub * (1.0 - lam_init)


def _attn_kernel(q_ref, k_ref, vt_ref, lam_ref, sub_ref, o_ref, acc_ref, st_ref,
                 *, tq, tk, lam_init):
    qi = pl.program_id(1)
    tv = vt_ref.shape[-1]
    n_sub = tk // tv
    q = q_ref[...].astype(F32)
    lane = lax.broadcasted_iota(jnp.int32, q.shape, 1)
    qs = jnp.concatenate([jnp.where(lane < HEAD_DIM, q, 0.0),
                          jnp.where(lane >= HEAD_DIM, q, 0.0)], axis=0).astype(BF16)
    acc_ref[...] = jnp.zeros(acc_ref.shape, F32)

    def scores(kc, j):
        k0 = pl.multiple_of(kc * tk + j * tv, tv)
        st_ref[j] = lax.dot_general(k_ref[pl.ds(k0, tv), :], qs, (((1,), (1,)), ((), ())),
                                    preferred_element_type=F32)

    def absorb(kc, j, m, l, masked):
        st = st_ref[j]
        if masked:
            key = lax.broadcasted_iota(jnp.int32, st.shape, 0) + (kc * tk + j * tv - qi * tq)
            qry = lax.broadcasted_iota(jnp.int32, st.shape, 1)
            qry = jnp.where(qry >= tq, qry - tq, qry)
            st = jnp.where(key <= qry, st, -jnp.inf)
        m_new = jnp.maximum(m, jnp.max(st, axis=0, keepdims=True))
        alpha = jnp.exp2(m - m_new)
        p = jnp.exp2(st - m_new)
        l = alpha * l + jnp.sum(p, axis=0, keepdims=True)
        pv = jnp.dot(vt_ref[kc * n_sub + j], p.astype(BF16), preferred_element_type=F32)
        acc_ref[...] = alpha * acc_ref[...] + pv
        return m_new, l

    def chunk(kc, m, l, masked):
        for j in range(n_sub):
            if j + 1 < n_sub:
                scores(kc, j + 1)
            elif not masked:
                scores(kc + 1, 0)
            m, l = absorb(kc, j, m, l, masked)
        return m, l

    n_full = (qi * tq) // tk
    scores(0, 0)
    init = (jnp.full((1, 2 * tq), -jnp.inf, F32), jnp.zeros((1, 2 * tq), F32))
    m, l = lax.fori_loop(0, n_full, lambda kc, c: chunk(kc, c[0], c[1], False), init)
    m, l = chunk(n_full, m, l, True)

    lam = _lambda(lam_ref, lam_init)
    o = acc_ref[...] / l
    out = o[:, :tq] - lam * o[:, tq:]
    ms = jnp.mean(out * out, axis=0, keepdims=True)
    y = out * lax.rsqrt(ms + EPS) * sub_ref[...] * (1.0 - lam_init)
    o_ref[...] = y.T.astype(o_ref.dtype)


def _attn_prompt(q, k, vt, lam_vec, subln, lam_init, tq, tk):
    s, width = q.shape
    tv = vt.shape[-1]
    tq = _tile(s, tq)
    tk = _tile(s, max(tk, tv))
    assert tk % tq == 0 and tk % tv == 0
    kern = functools.partial(_attn_kernel, tq=tq, tk=tk, lam_init=lam_init)
    return pl.pallas_call(
        kern,
        grid=(N_HEADS, s // tq),
        in_specs=[
            pl.BlockSpec((tq, V_DIM), lambda h, i: (i, h)),
            pl.BlockSpec((s, V_DIM), lambda h, i: (0, h)),
            pl.BlockSpec((None, s // tv, V_DIM, tv), lambda h, i: (h, 0, 0, 0)),
            pl.BlockSpec((4, HEAD_DIM), lambda h, i: (0, 0)),
            pl.BlockSpec((V_DIM, 1), lambda h, i: (0, 0)),
        ],
        out_specs=pl.BlockSpec((tq, V_DIM), lambda h, i: (i, h)),
        out_shape=jax.ShapeDtypeStruct((s, width), BF16),
        scratch_shapes=[pltpu.VMEM((V_DIM, 2 * tq), F32),
                        pltpu.VMEM((tk // tv, tv, 2 * tq), F32)],
        compiler_params=_params(("parallel", "arbitrary")),
        name="attn_prompt",
    )(q, k, vt, lam_vec, subln.reshape(V_DIM, 1))


def _attn_sample_kernel(pt_ref, *refs, n_pg, lam_init):
    k_refs = refs[:n_pg]
    v_refs = refs[n_pg:2 * n_pg]
    (q_ref, kn_ref, vn_ref, exp_ref, lam_ref, sub_ref, o_ref,
     qm_ref, m_ref, l_ref, acc_ref) = refs[2 * n_pg:]
    s_id = pl.program_id(1)
    n_rows = 2 * N_HEADS
    width = q_ref.shape[-1]
    page = k_refs[0].shape[1]

    @pl.when(s_id == 0)
    def _():
        q = jnp.broadcast_to(q_ref[0].astype(F32), (n_rows, width))
        r = lax.broadcasted_iota(jnp.int32, (n_rows, width), 0)
        c = lax.broadcasted_iota(jnp.int32, (n_rows, width), 1)
        keep = (c >= r * HEAD_DIM) & (c < (r + 1) * HEAD_DIM)
        qm_ref[...] = jnp.where(keep, q, 0.0).astype(BF16)
        m_ref[...] = jnp.full(m_ref.shape, -jnp.inf, F32)
        l_ref[...] = jnp.zeros(l_ref.shape, F32)
        acc_ref[...] = jnp.zeros(acc_ref.shape, F32)

    qm = qm_ref[...]
    s = jnp.concatenate(
        [jnp.dot(qm, k_refs[g][...].astype(BF16), preferred_element_type=F32)
         for g in range(n_pg)], axis=1)
    m_prev = m_ref[...]
    m_new = jnp.maximum(m_prev, jnp.max(s, axis=1, keepdims=True))
    alpha = jnp.exp2(m_prev - m_new)
    p = jnp.exp2(s - m_new)
    l_ref[...] = alpha * l_ref[...] + jnp.sum(p, axis=1, keepdims=True)
    m_ref[...] = m_new
    pb = p.astype(BF16)
    p_rows = jnp.concatenate([pb[:, g * page:(g + 1) * page] for g in range(n_pg)], axis=0)
    p_wide = jnp.dot(p_rows, exp_ref[...], preferred_element_type=F32)
    r = lax.broadcasted_iota(jnp.int32, p_wide.shape, 0)
    c = lax.broadcasted_iota(jnp.int32, p_wide.shape, 1)
    own = (c % N_HEADS) == ((r % n_rows) // 2)
    p_wide = jnp.where(own, p_wide, 0.0).astype(BF16)
    acc = alpha * acc_ref[...]
    for g in range(n_pg):
        acc = acc + jnp.dot(p_wide[g * n_rows:(g + 1) * n_rows], v_refs[g][...].astype(BF16),
                            preferred_element_type=F32)
    acc_ref[...] = acc

    @pl.when(s_id == pl.num_programs(1) - 1)
    def _():
        kn = kn_ref[0].astype(F32)
        vn = vn_ref[0].astype(F32)
        s_new = jnp.sum(qm.astype(F32) * kn, axis=1, keepdims=True)
        m_old = m_ref[...]
        m_fin = jnp.maximum(m_old, s_new)
        a = jnp.exp2(m_old - m_fin)
        p_new = jnp.exp2(s_new - m_fin)
        l_fin = a * l_ref[...] + p_new
        acc_fin = a * acc_ref[...] + p_new.astype(BF16).astype(F32) * vn
        o = acc_fin / l_fin
        o1 = jnp.concatenate([o[2 * h:2 * h + 1] for h in range(N_HEADS)], axis=0)
        o2 = jnp.concatenate([o[2 * h + 1:2 * h + 2] for h in range(N_HEADS)], axis=0)
        lam = _lambda(lam_ref, lam_init)
        o_ref[0] = _subln(o1 - lam * o2, sub_ref[...], lam_init)


def _attn_sample(q, k_new, v_new, ck, cv, page_idx, lam_vec, subln, lam_init, n_pg):
    db, width = q.shape
    page = ck.shape[2]
    n_pages = page_idx.shape[1]
    n_pg = _tile(n_pages, n_pg)
    n_rows = 2 * N_HEADS
    tok = jnp.arange(page)
    expand = (tok[:, None] == (jnp.arange(page * N_HEADS)[None, :] // N_HEADS)).astype(BF16)
    vn = jnp.repeat(v_new.reshape(db, N_HEADS, V_DIM), 2, axis=1)

    def page_spec(shape, g):
        return pl.BlockSpec((None,) + shape, lambda b, s, pt, g=g: (pt[b, s * n_pg + g], 0, 0))

    const = lambda b, s, pt: (0, 0)
    seq = lambda b, s, pt: (b, 0, 0)
    kern = functools.partial(_attn_sample_kernel, n_pg=n_pg, lam_init=lam_init)
    out = pl.pallas_call(
        kern,
        grid_spec=pltpu.PrefetchScalarGridSpec(
            num_scalar_prefetch=1,
            grid=(db, n_pages // n_pg),
            in_specs=([page_spec((width, page), g) for g in range(n_pg)]
                      + [page_spec((page * N_HEADS, V_DIM), g) for g in range(n_pg)]
                      + [pl.BlockSpec((1, 1, width), seq), pl.BlockSpec((1, 1, width), seq),
                         pl.BlockSpec((1, n_rows, V_DIM), seq),
                         pl.BlockSpec(expand.shape, const),
                         pl.BlockSpec((4, HEAD_DIM), const),
                         pl.BlockSpec((1, V_DIM), const)]),
            out_specs=pl.BlockSpec((1, N_HEADS, V_DIM), seq),
            scratch_shapes=[pltpu.VMEM((n_rows, width), BF16),
                            pltpu.VMEM((n_rows, 1), F32), pltpu.VMEM((n_rows, 1), F32),
                            pltpu.VMEM((n_rows, V_DIM), F32)],
        ),
        out_shape=jax.ShapeDtypeStruct((db, N_HEADS, V_DIM), F32),
        compiler_params=_params(("parallel", "arbitrary")),
        name="attn_sample",
    )(page_idx, *([ck] * n_pg), *([cv] * n_pg),
      q.reshape(db, 1, width), k_new.reshape(db, 1, width), vn, expand,
      lam_vec, subln.reshape(1, V_DIM))
    return out.reshape(db, width).astype(BF16)


def _s5_tables(a_re, a_im, b_re, b_im, c_re, c_im, d, log_dt):
    g, p = a_re.shape
    n_slab = g // S5_SLAB_GROUPS
    dt = jnp.exp(log_dt.astype(F32))[:, None]
    mag = jnp.exp(dt * a_re)
    ab_re = mag * jnp.cos(dt * a_im)
    ab_im = mag * jnp.sin(dt * a_im)
    den = a_re * a_re + a_im * a_im
    n_re = ab_re - 1.0
    f_re = (n_re * a_re + ab_im * a_im) / den
    f_im = (ab_im * a_re - n_re * a_im) / den
    bb_re = f_re[..., None] * b_re - f_im[..., None] * b_im
    bb_im = f_re[..., None] * b_im + f_im[..., None] * b_re
    eye = jnp.eye(S5_SLAB_GROUPS, dtype=F32)

    def in_mat(bb):
        bb = bb.reshape(n_slab, S5_SLAB_GROUPS, p, S5_CH)
        return jnp.einsum('jgpc,gh->jgchp', bb, eye).reshape(n_slab, V7X_LANES, S5_SLAB_STATE)

    def out_mat(cc):
        cc = cc.reshape(n_slab, S5_SLAB_GROUPS, S5_CH, p)
        return jnp.einsum('jgcp,gh->jgphc', cc, eye).reshape(n_slab, S5_SLAB_STATE, V7X_LANES)

    w_in = jnp.concatenate([in_mat(bb_re), in_mat(bb_im)], axis=2).astype(BF16)
    w_out = jnp.concatenate([out_mat(c_re.astype(F32)), out_mat(-c_im.astype(F32))],
                            axis=1).astype(BF16)

    pw_re, pw_im = [ab_re], [ab_im]
    for _ in range(V7X_SUBLANES - 1):
        r, i = pw_re[-1], pw_im[-1]
        pw_re.append(r * ab_re - i * ab_im)
        pw_im.append(r * ab_im + i * ab_re)
    flat = lambda t: t.reshape(n_slab, 1, S5_SLAB_STATE)
    p_re = jnp.concatenate([flat(t) for t in pw_re], axis=1)
    p_im = jnp.concatenate([flat(t) for t in pw_im], axis=1)
    rows = jnp.arange(V7X_SUBLANES)[None, :, None]
    steps = []
    for k in (1, 2, 4):
        steps.append(jnp.where(rows >= k, flat(pw_re[k - 1]), 0.0))
        steps.append(jnp.where(rows >= k, flat(pw_im[k - 1]), 0.0))
    scan = jnp.stack([p_re, p_im] + steps, axis=1)
    dvec = d.astype(F32).reshape(n_slab, 1, V7X_LANES)
    return w_in, w_out, scan, dvec


def _glu(y, wglu_ref, bglu_ref):
    g = jax.nn.gelu(y)
    z = jnp.dot(g.astype(BF16), wglu_ref[...], preferred_element_type=F32) + bglu_ref[...]
    return g * jax.nn.sigmoid(z)


def _s5_prompt_kernel(u_ref, win_ref, wout_ref, scan_ref, d_ref, wglu_ref, bglu_ref,
                      o_ref, st_ref, x_scr, y_scr, c_scr, *, tt):
    t = pl.program_id(0)
    n_slab = win_ref.shape[0]
    ns = S5_SLAB_STATE

    @pl.when(t == 0)
    def _():
        c_scr[...] = jnp.zeros(c_scr.shape, F32)

    for j in range(n_slab):
        lanes = slice(j * V7X_LANES, (j + 1) * V7X_LANES)
        u_j = u_ref[:, lanes]
        x_scr[...] = jnp.dot(u_j.astype(BF16), win_ref[j], preferred_element_type=F32)
        p_re, p_im = scan_ref[j, 0], scan_ref[j, 1]

        def body(i, carry):
            cr, ci = carry
            r0 = pl.multiple_of(i * V7X_SUBLANES, V7X_SUBLANES)
            xr = x_scr[pl.ds(r0, V7X_SUBLANES), 0:ns]
            xi = x_scr[pl.ds(r0, V7X_SUBLANES), ns:2 * ns]
            for n, k in enumerate((1, 2, 4)):
                mr, mi = scan_ref[j, 2 + 2 * n], scan_ref[j, 3 + 2 * n]
                sr = pltpu.roll(xr, k, 0)
                si = pltpu.roll(xi, k, 0)
                xr, xi = xr + mr * sr - mi * si, xi + mr * si + mi * sr
            xr, xi = xr + p_re * cr - p_im * ci, xi + p_re * ci + p_im * cr
            x_scr[pl.ds(r0, V7X_SUBLANES), 0:ns] = xr
            x_scr[pl.ds(r0, V7X_SUBLANES), ns:2 * ns] = xi
            return (xr[V7X_SUBLANES - 1:V7X_SUBLANES], xi[V7X_SUBLANES - 1:V7X_SUBLANES])

        cr, ci = lax.fori_loop(0, tt // V7X_SUBLANES, body,
                               (c_scr[j:j + 1, 0:ns], c_scr[j:j + 1, ns:2 * ns]), unroll=True)
        c_scr[j:j + 1, 0:ns] = cr
        c_scr[j:j + 1, ns:2 * ns] = ci
        y = jnp.dot(x_scr[...].astype(BF16), wout_ref[j], preferred_element_type=F32)
        y_scr[:, lanes] = y + d_ref[j] * u_j

    o_ref[...] = _glu(y_scr[...], wglu_ref, bglu_ref).astype(o_ref.dtype)
    st_ref[...] = c_scr[...]


def _s5_prompt(u, tabs, w_glu_b, b_glu, tt):
    s, width = u.shape
    w_in, w_out, scan, dvec = tabs
    n_slab = w_in.shape[0]
    tt = _tile(s, tt)
    full = lambda a: pl.BlockSpec(a.shape, lambda t: (0,) * a.ndim)
    bglu = b_glu.reshape(1, width).astype(F32)
    kern = functools.partial(_s5_prompt_kernel, tt=tt)
    out, st = pl.pallas_call(
        kern,
        grid=(s // tt,),
        in_specs=[pl.BlockSpec((tt, width), lambda t: (t, 0)),
                  full(w_in), full(w_out), full(scan), full(dvec), full(w_glu_b), full(bglu)],
        out_specs=[pl.BlockSpec((tt, width), lambda t: (t, 0)),
                   pl.BlockSpec((n_slab, 2 * S5_SLAB_STATE), lambda t: (0, 0))],
        out_shape=[jax.ShapeDtypeStruct((s, width), BF16),
                   jax.ShapeDtypeStruct((n_slab, 2 * S5_SLAB_STATE), F32)],
        scratch_shapes=[pltpu.VMEM((tt, 2 * S5_SLAB_STATE), F32),
                        pltpu.VMEM((tt, width), F32),
                        pltpu.VMEM((n_slab, 2 * S5_SLAB_STATE), F32)],
        compiler_params=_params(("arbitrary",)),
        name="s5_prompt",
    )(u, w_in, w_out, scan, dvec, w_glu_b, bglu)
    groups = n_slab * S5_SLAB_GROUPS
    s_re = st[:, :S5_SLAB_STATE].reshape(groups, S5_STATE)
    s_im = st[:, S5_SLAB_STATE:].reshape(groups, S5_STATE)
    return out, s_re, s_im


def _s5_sample_kernel(u_ref, sre_ref, sim_ref, win_ref, wout_ref, scan_ref, d_ref, wglu_ref,
                      bglu_ref, o_ref, xre_ref, xim_ref, y_scr):
    n_slab = win_ref.shape[0]
    ns = S5_SLAB_STATE
    for j in range(n_slab):
        lanes = slice(j * V7X_LANES, (j + 1) * V7X_LANES)
        st = slice(j * ns, (j + 1) * ns)
        u_j = u_ref[:, lanes]
        bu = jnp.dot(u_j.astype(BF16), win_ref[j], preferred_element_type=F32)
        ab_re, ab_im = scan_ref[j, 0, 0:1], scan_ref[j, 1, 0:1]
        s_re, s_im = sre_ref[:, st], sim_ref[:, st]
        xr = ab_re * s_re - ab_im * s_im + bu[:, 0:ns]
        xi = ab_re * s_im + ab_im * s_re + bu[:, ns:2 * ns]
        xre_ref[:, st] = xr
        xim_ref[:, st] = xi
        x = jnp.concatenate([xr, xi], axis=1).astype(BF16)
        y = jnp.dot(x, wout_ref[j], preferred_element_type=F32)
        y_scr[:, lanes] = y + d_ref[j] * u_j
    o_ref[...] = _glu(y_scr[...], wglu_ref, bglu_ref).astype(o_ref.dtype)


def _s5_sample(u, s_re, s_im, tabs, w_glu_b, b_glu):
    db, width = u.shape
    w_in, w_out, scan, dvec = tabs
    n_state = s_re.shape[1] * s_re.shape[2]
    bglu = b_glu.reshape(1, width).astype(F32)
    args = (u, s_re.reshape(db, n_state).astype(F32), s_im.reshape(db, n_state).astype(F32),
            w_in, w_out, scan, dvec, w_glu_b, bglu)
    full = lambda a: pl.BlockSpec(a.shape, lambda i: (0,) * a.ndim)
    out, x_re, x_im = pl.pallas_call(
        _s5_sample_kernel,
        grid=(1,),
        in_specs=[full(a) for a in args],
        out_specs=[pl.BlockSpec((db, width), lambda i: (0, 0)),
                   pl.BlockSpec((db, n_state), lambda i: (0, 0)),
                   pl.BlockSpec((db, n_state), lambda i: (0, 0))],
        out_shape=[jax.ShapeDtypeStruct((db, width), BF16),
                   jax.ShapeDtypeStruct((db, n_state), F32),
                   jax.ShapeDtypeStruct((db, n_state), F32)],
        scratch_shapes=[pltpu.VMEM((db, width), F32)],
        compiler_params=_params(("arbitrary",)),
        name="s5_sample",
    )(*args)
    return out, x_re.reshape(s_re.shape), x_im.reshape(s_im.shape)


def _outproj_kernel(x_ref, a_ref, s_ref, w_ref, g_ref, h_ref, f_ref):
    half = a_ref.shape[1]
    h = (x_ref[...]
         + jnp.dot(a_ref[...], w_ref[0:half, :], preferred_element_type=F32)
         + jnp.dot(s_ref[...], w_ref[half:, :], preferred_element_type=F32))
    h_ref[...] = h
    ms = jnp.mean(h * h, axis=-1, keepdims=True)
    f_ref[...] = (h * lax.rsqrt(ms + EPS) * g_ref[...]).astype(BF16)


def _outproj(x, att, s5o, w_out_b, norm_ffn, tm):
    m, d = x.shape
    half = att.shape[1]
    tm = _tile(m, tm)
    row = lambda i: (i, 0)
    const = lambda i: (0, 0)
    return pl.pallas_call(
        _outproj_kernel,
        grid=(m // tm,),
        in_specs=[pl.BlockSpec((tm, d), row), pl.BlockSpec((tm, half), row),
                  pl.BlockSpec((tm, half), row), pl.BlockSpec((d, d), const),
                  pl.BlockSpec((1, d), const)],
        out_specs=[pl.BlockSpec((tm, d), row), pl.BlockSpec((tm, d), row)],
        out_shape=[jax.ShapeDtypeStruct((m, d), F32), jax.ShapeDtypeStruct((m, d), BF16)],
        compiler_params=_params(("parallel",)),
        name="out_proj",
    )(x, att, s5o, w_out_b, norm_ffn.reshape(1, d))


def _ffn_init(h_ref, o_ref):
    @pl.when(pl.program_id(1) == 0)
    def _():
        o_ref[...] = h_ref[...]


def _ffn_down(acts, wd_ref, o_ref):
    tc = acts[0].shape[1]
    part = jnp.dot(acts[0], wd_ref[0:tc, :], preferred_element_type=F32)
    for n in range(1, len(acts)):
        part = part + jnp.dot(acts[n], wd_ref[n * tc:(n + 1) * tc, :],
                              preferred_element_type=F32)
    o_ref[...] += part


def _ffn_prompt_kernel(f_ref, h_ref, wg_ref, wu_ref, wd_ref, cw_ref, cb_ref,
                       o_ref, buf_ref, hgx_ref, prev_ref, *, tm, tc):
    i = pl.program_id(0)
    j = pl.program_id(1)
    halo = V7X_SUBLANES
    tf = wg_ref.shape[1]
    _ffn_init(h_ref, o_ref)

    @pl.when(i == 0)
    def _():
        hgx_ref[0:halo, :] = jnp.zeros((halo, tf), F32)

    @pl.when(i > 0)
    def _():
        hgx_ref[0:halo, :] = prev_ref[j]

    f = f_ref[...]
    blocks = [slice(c0, c0 + tc) for c0 in range(0, tf, tc)]
    hgs = [None] * len(blocks)
    hus = [None] * len(blocks)

    def up(n):
        hgs[n] = jnp.dot(f, wg_ref[:, blocks[n]], preferred_element_type=F32)
        hus[n] = jnp.dot(f, wu_ref[:, blocks[n]], preferred_element_type=F32)

    up(0)
    part = None
    for n, cs in enumerate(blocks):
        if n + 1 < len(blocks):
            up(n + 1)
        hg = hgs[n]
        hgx_ref[halo:, cs] = hg
        cw = cw_ref[:, cs]
        conv = cb_ref[:, cs] + cw[2:3] * hg
        for tap in range(CONV_W - 1):
            off = halo - (CONV_W - 1) + tap
            conv = conv + cw[tap:tap + 1] * hgx_ref[pl.ds(off, tm), cs]
        act = (jax.nn.gelu(conv) * hus[n]).astype(BF16)
        down = jnp.dot(act, wd_ref[cs, :], preferred_element_type=F32)
        part = down if part is None else part + down
    tail = hgx_ref[tm:tm + halo, :]
    prev_ref[j] = tail
    buf_ref[...] = tail
    o_ref[...] += part


def _ffn_sample_kernel(f_ref, h_ref, wg_ref, wu_ref, wd_ref, cw_ref, cb_ref, b0_ref, b1_ref,
                       o_ref, hg_ref):
    _ffn_init(h_ref, o_ref)
    f = f_ref[...]
    hg = jnp.dot(f, wg_ref[...], preferred_element_type=F32)
    hu = jnp.dot(f, wu_ref[...], preferred_element_type=F32)
    hg_ref[...] = hg
    cw = cw_ref[...]
    conv = cb_ref[...] + cw[0:1] * b0_ref[...] + cw[1:2] * b1_ref[...] + cw[2:3] * hg
    _ffn_down([(jax.nn.gelu(conv) * hu).astype(BF16)], wd_ref, o_ref)


def _ffn(f, h, w_gate_b, w_up_b, w_down_b, conv_w, conv_b, conv_buf, tm, tf):
    m, d = f.shape
    dff = w_gate_b.shape[1]
    tm = _tile(m, tm)
    tf = _tile(dff, tf)
    n_ff = dff // tf
    row = lambda i, j: (i, 0)
    col = lambda i, j: (0, j)
    common_in = [pl.BlockSpec((tm, d), row), pl.BlockSpec((tm, d), row),
                 pl.BlockSpec((d, tf), col), pl.BlockSpec((d, tf), col),
                 pl.BlockSpec((tf, d), lambda i, j: (j, 0)),
                 pl.BlockSpec((CONV_W, tf), col), pl.BlockSpec((1, tf), col)]
    common_args = (f, h, w_gate_b, w_up_b, w_down_b, conv_w.astype(F32),
                   conv_b.reshape(1, dff).astype(F32))
    if conv_buf is None:
        halo = V7X_SUBLANES
        kern = functools.partial(_ffn_prompt_kernel, tm=tm, tc=_tile(tf, V7X_MXU_COLS))
        out, tail = pl.pallas_call(
            kern,
            grid=(m // tm, n_ff),
            in_specs=common_in,
            out_specs=[pl.BlockSpec((tm, d), row),
                       pl.BlockSpec((None, halo, tf), lambda i, j: (i, 0, j))],
            out_shape=[jax.ShapeDtypeStruct((m, d), F32),
                       jax.ShapeDtypeStruct((m // tm, halo, dff), F32)],
            scratch_shapes=[pltpu.VMEM((tm + halo, tf), F32),
                            pltpu.VMEM((n_ff, halo, tf), F32)],
            compiler_params=_params(("arbitrary", "arbitrary")),
            name="ffn_prompt",
        )(*common_args)
        return out, tail[-1, halo - (CONV_W - 1):]
    b0 = conv_buf[:, 0, :].astype(F32)
    b1 = conv_buf[:, 1, :].astype(F32)
    out, hg = pl.pallas_call(
        _ffn_sample_kernel,
        grid=(m // tm, n_ff),
        in_specs=common_in + [pl.BlockSpec((tm, tf), lambda i, j: (i, j))] * 2,
        out_specs=[pl.BlockSpec((tm, d), row), pl.BlockSpec((tm, tf), lambda i, j: (i, j))],
        out_shape=[jax.ShapeDtypeStruct((m, d), F32), jax.ShapeDtypeStruct((m, dff), F32)],
        compiler_params=_params(("parallel", "arbitrary")),
        name="ffn_sample",
    )(*common_args, b0, b1)
    return out, jnp.stack([b1, hg], axis=1)


def _ple_kernel(h_ref, p_ref, wg_ref, wp_ref, g_ref, o_ref):
    h = h_ref[...]
    gate = jax.nn.sigmoid(jnp.dot(h.astype(BF16), wg_ref[...], preferred_element_type=F32))
    e = jnp.dot(p_ref[...].astype(BF16), wp_ref[...], preferred_element_type=F32)
    ms = jnp.mean(e * e, axis=-1, keepdims=True)
    o_ref[...] = h + gate * (e * lax.rsqrt(ms + EPS) * g_ref[...])


def _ple(h, p, w_gate_b, w_proj_b, ple_norm, tm):
    m, d = h.shape
    pd = p.shape[1]
    tm = _tile(m, tm)
    row = lambda i: (i, 0)
    const = lambda i: (0, 0)
    return pl.pallas_call(
        _ple_kernel,
        grid=(m // tm,),
        in_specs=[pl.BlockSpec((tm, d), row), pl.BlockSpec((tm, pd), row),
                  pl.BlockSpec((d, d), const), pl.BlockSpec((pd, d), const),
                  pl.BlockSpec((1, d), const)],
        out_specs=pl.BlockSpec((tm, d), row),
        out_shape=jax.ShapeDtypeStruct((m, d), F32),
        compiler_params=_params(("parallel",)),
        name="ple_gate",
    )(h, p, w_gate_b, w_proj_b, ple_norm.reshape(1, d))


def kernel(x_prompt, x_sample, cache_k, cache_v, state_s5_re, state_s5_im, state_conv, page_table, p_prompt, p_sample, norm_mix, w_in, q_norm, k_norm, lam_q1, lam_k1, lam_q2, lam_k2, subln, s5_a_re, s5_a_im, s5_b_re, s5_b_im, s5_c_re, s5_c_im, s5_d, s5_log_dt, w_glu, b_glu, w_out, norm_ffn, w_gate, w_up, conv_w, conv_b, w_down, w_ple_gate, w_ple_proj, ple_norm):
    depth = w_in.shape[0]
    b, s, d = x_prompt.shape
    db, t_new, _ = x_sample.shape
    assert b == 1 and t_new == 1
    page = cache_k.shape[2]
    past_len = page_table.shape[1] * page
    hp = x_prompt.reshape(s, d)
    hs = x_sample.reshape(db, d)
    pos_p = jnp.arange(s)
    pos_s = jnp.full((db,), past_len)
    n_pool = cache_k.shape[1]
    ck = jnp.transpose(cache_k, (0, 1, 3, 4, 5, 2)).reshape(depth * n_pool, -1, page)
    cv = cache_v.reshape(depth * n_pool, page * N_HEADS, V_DIM)
    outs = [[] for _ in range(10)]
    for i in range(depth):
        lam_init = 0.8 - 0.6 * math.exp(-0.3 * i)
        lam_vec = jnp.stack([lam_q1[i], lam_k1[i], lam_q2[i], lam_k2[i]]).astype(F32)
        w_in_b = w_in[i].astype(BF16)
        w_glu_b = w_glu[i].astype(BF16)
        w_out_b = w_out[i].astype(BF16)
        w_gate_b = w_gate[i].astype(BF16)
        w_up_b = w_up[i].astype(BF16)
        w_down_b = w_down[i].astype(BF16)
        w_pg_b = w_ple_gate[i].astype(BF16)
        w_pp_b = w_ple_proj[i].astype(BF16)
        tabs = _s5_tables(s5_a_re[i], s5_a_im[i], s5_b_re[i], s5_b_im[i], s5_c_re[i],
                          s5_c_im[i], s5_d[i], s5_log_dt[i])

        q, k32, kb, v32, vt, u = _project(hp, pos_p, norm_mix[i], w_in_b, q_norm[i], k_norm[i],
                                          512, True)
        att = _attn_prompt(q, kb, vt, lam_vec, subln[i], lam_init, 512, 1024)
        s5o, s_re, s_im = _s5_prompt(u, tabs, w_glu_b, b_glu[i], 256)
        h1, f = _outproj(hp, att, s5o, w_out_b, norm_ffn[i], 512)
        h2, cbuf = _ffn(f, h1, w_gate_b, w_up_b, w_down_b, conv_w[i], conv_b[i], None, 512, 512)
        hp = _ple(h2, p_prompt[i].reshape(s, -1), w_pg_b, w_pp_b, ple_norm[i], 512)
        outs[0].append(k32.reshape(b, s, N_HEADS, 2, HEAD_DIM))
        outs[1].append(v32.reshape(b, s, N_HEADS, V_DIM))
        outs[2].append(s_re[None])
        outs[3].append(s_im[None])
        outs[4].append(cbuf[None])

        q, k32, kb, v32, vb, u = _project(hs, pos_s, norm_mix[i], w_in_b, q_norm[i], k_norm[i],
                                          512, False)
        att = _attn_sample(q, kb, vb, ck, cv, page_table + i * n_pool, lam_vec, subln[i],
                           lam_init, 8)
        s5o, s_re, s_im = _s5_sample(u, state_s5_re[i], state_s5_im[i], tabs, w_glu_b, b_glu[i])
        h1, f = _outproj(hs, att, s5o, w_out_b, norm_ffn[i], 512)
        h2, cbuf = _ffn(f, h1, w_gate_b, w_up_b, w_down_b, conv_w[i], conv_b[i], state_conv[i],
                        512, 512)
        hs = _ple(h2, p_sample[i].reshape(db, -1), w_pg_b, w_pp_b, ple_norm[i], 512)
        outs[5].append(k32.reshape(db, t_new, N_HEADS, 2, HEAD_DIM))
        outs[6].append(v32.reshape(db, t_new, N_HEADS, V_DIM))
        outs[7].append(s_re)
        outs[8].append(s_im)
        outs[9].append(cbuf)
    return (hp.reshape(b, s, d), hs.reshape(db, t_new, d)) + tuple(jnp.stack(o) for o in outs)
```

```python
import functools
import math

import jax
import jax.numpy as jnp
from jax import lax
from jax.experimental import pallas as pl
from jax.experimental.pallas import tpu as pltpu

N_HEADS = 8
HEAD_DIM = 64
V_DIM = 2 * HEAD_DIM
ROT_DIM = HEAD_DIM // 4
ROPE_THETA = 500000.0
S5_CH = 16
S5_STATE = 64
CONV_W = 3
EPS = 1e-6
Q_SCALE = math.log2(math.e) * HEAD_DIM ** -0.5

V7X_LANES = 128
V7X_SUBLANES = 8
V7X_MXU_COLS = 256
VMEM_LIMIT_BYTES = 56 * 1024 * 1024
ATTN_VMEM_LIMIT_BYTES = 60 * 1024 * 1024

S5_SLAB_GROUPS = V7X_LANES // S5_CH
S5_SLAB_STATE = S5_SLAB_GROUPS * S5_STATE
S5_TILE = 256

F32 = jnp.float32
BF16 = jnp.bfloat16


def _tile(n, pref):
    t = min(n, pref)
    assert n % t == 0, (n, t)
    return t


def _params(sem, vmem=VMEM_LIMIT_BYTES):
    return pltpu.CompilerParams(dimension_semantics=sem, vmem_limit_bytes=vmem)


def _qk_post(p, gain, cos, sin_lo, sin_hi, seg):
    outs = []
    for c in range(p.shape[1] // V7X_LANES):
        pc = p[:, c * V7X_LANES:(c + 1) * V7X_LANES]
        ms = jnp.dot((pc * pc).astype(BF16), seg, preferred_element_type=F32)
        y = pc * lax.rsqrt(ms + EPS) * gain
        y = (y * cos
             + pltpu.roll(y, V7X_LANES - ROT_DIM // 2, 1) * sin_lo
             + pltpu.roll(y, ROT_DIM // 2, 1) * sin_hi)
        outs.append(y)
    return jnp.concatenate(outs, axis=1)


def _proj_kernel(x_ref, g_ref, w_ref, qg_ref, kg_ref, cos_ref, slo_ref, shi_ref, seg_ref,
                 q_ref, k32_ref, kb_ref, v32_ref, vb_ref, u_ref, *, v_transposed, bw):
    width = q_ref.shape[1]
    nb = width // bw
    x = x_ref[...]
    ms = jnp.mean(x * x, axis=-1, keepdims=True)
    xn = (x * lax.rsqrt(ms + EPS) * g_ref[...]).astype(BF16)
    rope = (cos_ref[...], slo_ref[...], shi_ref[...], seg_ref[...])

    def block(c):
        return jnp.dot(xn, w_ref[:, c * bw:(c + 1) * bw], preferred_element_type=F32)

    for b in range(nb):
        cols = slice(b * bw, (b + 1) * bw)
        q = _qk_post(block(b), qg_ref[...], *rope)
        q_ref[:, cols] = (q * Q_SCALE).astype(BF16)
    for b in range(nb):
        cols = slice(b * bw, (b + 1) * bw)
        k = _qk_post(block(nb + b), kg_ref[...], *rope)
        k32_ref[:, cols] = k
        kb_ref[:, cols] = k.astype(BF16)
    for b in range(nb):
        cols = slice(b * bw, (b + 1) * bw)
        v = block(2 * nb + b)
        v32_ref[:, cols] = v
        if v_transposed:
            for hh in range(bw // V_DIM):
                vb_ref[b * (bw // V_DIM) + hh, 0] = (
                    v[:, hh * V_DIM:(hh + 1) * V_DIM].T.astype(BF16))
        else:
            vb_ref[:, cols] = v.astype(BF16)
    for b in range(nb):
        u_ref[:, b * bw:(b + 1) * bw] = block(3 * nb + b)


def _rope_tables(pos):
    half = ROT_DIM // 2
    inv = ROPE_THETA ** (-(jnp.arange(half, dtype=F32) * 2.0 / ROT_DIM))
    ang = pos.astype(F32)[:, None] * inv[None, :]
    cos, sin = jnp.cos(ang), jnp.sin(ang)
    t = pos.shape[0]
    ones = jnp.ones((t, HEAD_DIM - ROT_DIM), F32)
    zeros = jnp.zeros((t, HEAD_DIM - ROT_DIM), F32)
    zh = jnp.zeros((t, half), F32)
    cos_t = jnp.concatenate([cos, cos, ones], axis=1)
    lo_t = jnp.concatenate([-sin, zh, zeros], axis=1)
    hi_t = jnp.concatenate([zh, sin, zeros], axis=1)
    rep = V7X_LANES // HEAD_DIM
    return (jnp.tile(cos_t, (1, rep)), jnp.tile(lo_t, (1, rep)), jnp.tile(hi_t, (1, rep)))


def _project(x, pos, norm_mix, w_in_b, q_norm, k_norm, tm, v_transposed):
    m, d = x.shape
    width = w_in_b.shape[1] // 4
    tm = _tile(m, tm)
    if v_transposed:
        vb_spec = pl.BlockSpec((N_HEADS, 1, V_DIM, tm), lambda i: (0, i, 0, 0))
        vb_shape = jax.ShapeDtypeStruct((N_HEADS, m // tm, V_DIM, tm), BF16)
    else:
        vb_spec = pl.BlockSpec((tm, width), lambda i: (i, 0))
        vb_shape = jax.ShapeDtypeStruct((m, width), BF16)
    cos_t, lo_t, hi_t = _rope_tables(pos)
    rep = V7X_LANES // HEAD_DIM
    qg = jnp.tile(q_norm.reshape(1, HEAD_DIM), (1, rep))
    kg = jnp.tile(k_norm.reshape(1, HEAD_DIM), (1, rep))
    lane = jnp.arange(V7X_LANES)
    seg = ((lane[:, None] // HEAD_DIM) == (lane[None, :] // HEAD_DIM)).astype(BF16) / HEAD_DIM
    row = lambda i: (i, 0)
    const = lambda i: (0, 0)
    out_spec = pl.BlockSpec((tm, width), row)
    kern = functools.partial(_proj_kernel, v_transposed=v_transposed,
                             bw=_tile(width, 2 * V7X_MXU_COLS))
    return pl.pallas_call(
        kern,
        grid=(m // tm,),
        in_specs=[
            pl.BlockSpec((tm, d), row),
            pl.BlockSpec((1, d), const),
            pl.BlockSpec((d, 4 * width), const, pipeline_mode=pl.Buffered(1)),
            pl.BlockSpec((1, V7X_LANES), const),
            pl.BlockSpec((1, V7X_LANES), const),
            pl.BlockSpec((tm, V7X_LANES), row),
            pl.BlockSpec((tm, V7X_LANES), row),
            pl.BlockSpec((tm, V7X_LANES), row),
            pl.BlockSpec((V7X_LANES, V7X_LANES), const),
        ],
        out_specs=[out_spec] * 4 + [vb_spec, out_spec],
        out_shape=[
            jax.ShapeDtypeStruct((m, width), BF16),
            jax.ShapeDtypeStruct((m, width), F32),
            jax.ShapeDtypeStruct((m, width), BF16),
            jax.ShapeDtypeStruct((m, width), F32),
            vb_shape,
            jax.ShapeDtypeStruct((m, width), F32),
        ],
        compiler_params=_params(("parallel",)),
        name="in_proj",
    )(x, norm_mix.reshape(1, d), w_in_b, qg, kg, cos_t, lo_t, hi_t, seg)


def _lambda(lam_ref, lam_init):
    lv = lam_ref[...]
    l1 = jnp.sum(lv[0:1] * lv[1:2], axis=-1, keepdims=True)
    l2 = jnp.sum(lv[2:3] * lv[3:4], axis=-1, keepdims=True)
    return jnp.exp(l1) - jnp.exp(l2) + lam_init


def _subln(out, sub, lam_init):
    ms = jnp.mean(out * out, axis=-1, keepdims=True)
    return out * lax.rsqrt(ms + EPS) * sub * (1.0 - lam_init)


def _prompt_chunk(q_ref, k_ref, vt_ref, acc_ref, st_ref, ml_ref, qi, kc, *, tq, masked):
    tv = vt_ref.shape[-1]
    n_sub = tq // tv
    q = q_ref[...].astype(F32)
    lane = lax.broadcasted_iota(jnp.int32, q.shape, 1)
    qs = jnp.concatenate([jnp.where(lane < HEAD_DIM, q, 0.0),
                          jnp.where(lane >= HEAD_DIM, q, 0.0)], axis=0).astype(BF16)

    def scores(j):
        k0 = pl.multiple_of(kc * tq + j * tv, tv)
        st_ref[j] = lax.dot_general(k_ref[pl.ds(k0, tv), :], qs, (((1,), (1,)), ((), ())),
                                    preferred_element_type=F32)

    m, l = ml_ref[0:1, :], ml_ref[1:2, :]
    scores(0)
    for j in range(n_sub):
        if j + 1 < n_sub:
            scores(j + 1)
        st = st_ref[j]
        if masked:
            key = lax.broadcasted_iota(jnp.int32, st.shape, 0) + j * tv
            qry = lax.broadcasted_iota(jnp.int32, st.shape, 1)
            qry = jnp.where(qry >= tq, qry - tq, qry)
            st = jnp.where(key <= qry, st, -jnp.inf)
        m_new = jnp.maximum(m, jnp.max(st, axis=0, keepdims=True))
        alpha = jnp.exp2(m - m_new)
        p = jnp.exp2(st - m_new)
        l = alpha * l + jnp.sum(p, axis=0, keepdims=True)
        m = m_new
        pv = jnp.dot(vt_ref[kc * n_sub + j], p.astype(BF16), preferred_element_type=F32)
        acc_ref[...] = alpha * acc_ref[...] + pv
    ml_ref[0:1, :] = m
    ml_ref[1:2, :] = l


def _prompt_finish(acc_ref, ml_ref, lam_ref, sub_ref, o_ref, *, tq, lam_init):
    lam = _lambda(lam_ref, lam_init)
    o = acc_ref[...] / ml_ref[1:2, :]
    out = o[:, :tq] - lam * o[:, tq:]
    ms = jnp.mean(out * out, axis=0, keepdims=True)
    y = out * lax.rsqrt(ms + EPS) * sub_ref[...] * (1.0 - lam_init)
    o_ref[...] = y.T.astype(o_ref.dtype)


def _sample_pages(k_refs, v_refs, q_ref, kn_ref, vn_ref, exp_ref, lam_ref, sub_ref, o_ref,
                  qm_ref, m_ref, l_ref, acc_ref, first, last, *, lam_init):
    n_pg = len(k_refs)
    n_rows = 2 * N_HEADS
    width = q_ref.shape[-1]
    page = k_refs[0].shape[1]

    @pl.when(first)
    def _():
        q = jnp.broadcast_to(q_ref[0].astype(F32), (n_rows, width))
        r = lax.broadcasted_iota(jnp.int32, (n_rows, width), 0)
        c = lax.broadcasted_iota(jnp.int32, (n_rows, width), 1)
        keep = (c >= r * HEAD_DIM) & (c < (r + 1) * HEAD_DIM)
        qm_ref[...] = jnp.where(keep, q, 0.0).astype(BF16)
        m_ref[...] = jnp.full(m_ref.shape, -jnp.inf, F32)
        l_ref[...] = jnp.zeros(l_ref.shape, F32)
        acc_ref[...] = jnp.zeros(acc_ref.shape, F32)

    qm = qm_ref[...]
    s = jnp.concatenate(
        [jnp.dot(qm, k_refs[g][...].astype(BF16), preferred_element_type=F32)
         for g in range(n_pg)], axis=1)
    m_prev = m_ref[...]
    m_new = jnp.maximum(m_prev, jnp.max(s, axis=1, keepdims=True))
    alpha = jnp.exp2(m_prev - m_new)
    p = jnp.exp2(s - m_new)
    l_ref[...] = alpha * l_ref[...] + jnp.sum(p, axis=1, keepdims=True)
    m_ref[...] = m_new
    pb = p.astype(BF16)
    p_rows = jnp.concatenate([pb[:, g * page:(g + 1) * page] for g in range(n_pg)], axis=0)
    p_wide = jnp.dot(p_rows, exp_ref[...], preferred_element_type=F32)
    r = lax.broadcasted_iota(jnp.int32, p_wide.shape, 0)
    c = lax.broadcasted_iota(jnp.int32, p_wide.shape, 1)
    own = (c % N_HEADS) == ((r % n_rows) // 2)
    p_wide = jnp.where(own, p_wide, 0.0).astype(BF16)
    acc = alpha * acc_ref[...]
    for g in range(n_pg):
        acc = acc + jnp.dot(p_wide[g * n_rows:(g + 1) * n_rows], v_refs[g][...].astype(BF16),
                            preferred_element_type=F32)
    acc_ref[...] = acc

    @pl.when(last)
    def _():
        kn = kn_ref[0].astype(F32)
        vn = vn_ref[0].astype(F32)
        s_new = jnp.sum(qm.astype(F32) * kn, axis=1, keepdims=True)
        m_old = m_ref[...]
        m_fin = jnp.maximum(m_old, s_new)
        a = jnp.exp2(m_old - m_fin)
        p_new = jnp.exp2(s_new - m_fin)
        l_fin = a * l_ref[...] + p_new
        acc_fin = a * acc_ref[...] + p_new.astype(BF16).astype(F32) * vn
        o = acc_fin / l_fin
        o1 = jnp.concatenate([o[2 * h:2 * h + 1] for h in range(N_HEADS)], axis=0)
        o2 = jnp.concatenate([o[2 * h + 1:2 * h + 2] for h in range(N_HEADS)], axis=0)
        lam = _lambda(lam_ref, lam_init)
        o_ref[0] = _subln(o1 - lam * o2, sub_ref[...], lam_init)


def _attn_kernel(sh_ref, sq_ref, sk_ref, pt_ref, *refs, n_pg, tq, lam_init, n_sample_steps,
                 steps_per_seq):
    del sh_ref, pt_ref
    k_pages = refs[:n_pg]
    v_pages = refs[n_pg:2 * n_pg]
    (q_ref, k_ref, vt_ref, lam_ref, subc_ref, qs_ref, kn_ref, vn_ref, exp_ref, subr_ref,
     o_ref, os_ref, acc_ref, st_ref, ml_ref, qm_ref, ms_ref, ls_ref, accs_ref) = refs[2 * n_pg:]
    t = pl.program_id(0)
    qi = sq_ref[t]
    kc = sk_ref[t]

    @pl.when(kc == 0)
    def _():
        acc_ref[...] = jnp.zeros(acc_ref.shape, F32)
        ml_ref[0:1, :] = jnp.full((1, ml_ref.shape[1]), -jnp.inf, F32)
        ml_ref[1:2, :] = jnp.zeros((1, ml_ref.shape[1]), F32)

    chunk = functools.partial(_prompt_chunk, q_ref, k_ref, vt_ref, acc_ref, st_ref, ml_ref,
                              qi, kc, tq=tq)

    @pl.when(kc < qi)
    def _():
        chunk(masked=False)

    @pl.when(kc == qi)
    def _():
        chunk(masked=True)
        _prompt_finish(acc_ref, ml_ref, lam_ref, subc_ref, o_ref, tq=tq, lam_init=lam_init)

    @pl.when(t < n_sample_steps)
    def _():
        s_id = t % steps_per_seq
        _sample_pages(k_pages, v_pages, qs_ref, kn_ref, vn_ref, exp_ref, lam_ref, subr_ref,
                      os_ref, qm_ref, ms_ref, ls_ref, accs_ref, s_id == 0,
                      s_id == steps_per_seq - 1, lam_init=lam_init)


def _attention(q, k, vt, qs, k_new, v_new, ck, cv, page_idx, lam_vec, subln, lam_init, tq):
    s, width = q.shape
    db = qs.shape[0]
    tv = vt.shape[-1]
    tq = _tile(s, max(tq, tv))
    assert tq % tv == 0
    nq = s // tq
    page = ck.shape[2]
    n_pages = page_idx.shape[1]
    n_rows = 2 * N_HEADS

    sched = [(h, qi, kc) for h in range(N_HEADS) for qi in range(nq) for kc in range(qi + 1)]
    n_steps = len(sched)
    sh, sq, sk = (jnp.asarray([e[n] for e in sched], jnp.int32) for n in range(3))
    n_pg = next(g for g in range(1, n_pages + 1)
                if n_pages % g == 0 and db * (n_pages // g) <= n_steps)
    steps_per_seq = n_pages // n_pg
    n_sample_steps = db * steps_per_seq

    tok = jnp.arange(page)
    expand = (tok[:, None] == (jnp.arange(page * N_HEADS)[None, :] // N_HEADS)).astype(BF16)
    vn = jnp.repeat(v_new.reshape(db, N_HEADS, V_DIM), 2, axis=1)

    def seq_of(t):
        ts = jnp.minimum(t, n_sample_steps - 1)
        return ts // steps_per_seq, ts % steps_per_seq

    def page_spec(shape, g):
        def index(t, sh, sq, sk, pt):
            b, s_id = seq_of(t)
            return (pt[b, s_id * n_pg + g], 0, 0)
        return pl.BlockSpec((None,) + shape, index)

    const = lambda t, sh, sq, sk, pt: (0, 0)
    tile = lambda t, sh, sq, sk, pt: (sq[t], sh[t])
    seq = lambda t, sh, sq, sk, pt: (seq_of(t)[0], 0, 0)
    kern = functools.partial(_attn_kernel, n_pg=n_pg, tq=tq, lam_init=lam_init,
                             n_sample_steps=n_sample_steps, steps_per_seq=steps_per_seq)
    att, att_s = pl.pallas_call(
        kern,
        grid_spec=pltpu.PrefetchScalarGridSpec(
            num_scalar_prefetch=4,
            grid=(n_steps,),
            in_specs=([page_spec((width, page), g) for g in range(n_pg)]
                      + [page_spec((page * N_HEADS, V_DIM), g) for g in range(n_pg)]
                      + [pl.BlockSpec((tq, V_DIM), tile),
                         pl.BlockSpec((s, V_DIM), lambda t, sh, sq, sk, pt: (0, sh[t])),
                         pl.BlockSpec((None, s // tv, V_DIM, tv),
                                      lambda t, sh, sq, sk, pt: (sh[t], 0, 0, 0)),
                         pl.BlockSpec((4, HEAD_DIM), const),
                         pl.BlockSpec((V_DIM, 1), const),
                         pl.BlockSpec((1, 1, width), seq), pl.BlockSpec((1, 1, width), seq),
                         pl.BlockSpec((1, n_rows, V_DIM), seq),
                         pl.BlockSpec(expand.shape, const),
                         pl.BlockSpec((1, V_DIM), const)]),
            out_specs=[pl.BlockSpec((tq, V_DIM), tile),
                       pl.BlockSpec((1, N_HEADS, V_DIM), seq)],
            scratch_shapes=[pltpu.VMEM((V_DIM, 2 * tq), F32),
                            pltpu.VMEM((tq // tv, tv, 2 * tq), F32),
                            pltpu.VMEM((2, 2 * tq), F32),
                            pltpu.VMEM((n_rows, width), BF16),
                            pltpu.VMEM((n_rows, 1), F32), pltpu.VMEM((n_rows, 1), F32),
                            pltpu.VMEM((n_rows, V_DIM), F32)],
        ),
        out_shape=[jax.ShapeDtypeStruct((s, width), BF16),
                   jax.ShapeDtypeStruct((db, N_HEADS, V_DIM), F32)],
        compiler_params=_params(("arbitrary",), ATTN_VMEM_LIMIT_BYTES),
        name="attention",
    )(sh, sq, sk, page_idx, *([ck] * n_pg), *([cv] * n_pg),
      q, k, vt, lam_vec, subln.reshape(V_DIM, 1),
      qs.reshape(db, 1, width), k_new.reshape(db, 1, width), vn, expand,
      subln.reshape(1, V_DIM))
    return att, att_s.reshape(db, width).astype(BF16)


def _s5_tables(a_re, a_im, b_re, b_im, c_re, c_im, d, log_dt, seg_len):
    g, p = a_re.shape
    n_slab = g // S5_SLAB_GROUPS
    dt = jnp.exp(log_dt.astype(F32))[:, None]
    mag = jnp.exp(dt * a_re)
    ab_re = mag * jnp.cos(dt * a_im)
    ab_im = mag * jnp.sin(dt * a_im)
    den = a_re * a_re + a_im * a_im
    n_re = ab_re - 1.0
    f_re = (n_re * a_re + ab_im * a_im) / den
    f_im = (ab_im * a_re - n_re * a_im) / den
    bb_re = f_re[..., None] * b_re - f_im[..., None] * b_im
    bb_im = f_re[..., None] * b_im + f_im[..., None] * b_re
    eye = jnp.eye(S5_SLAB_GROUPS, dtype=F32)

    def in_mat(bb):
        bb = bb.reshape(n_slab, S5_SLAB_GROUPS, p, S5_CH)
        return jnp.einsum('jgpc,gh->jgchp', bb, eye).reshape(n_slab, V7X_LANES, S5_SLAB_STATE)

    def out_mat(cc):
        cc = cc.reshape(n_slab, S5_SLAB_GROUPS, S5_CH, p)
        return jnp.einsum('jgcp,gh->jgphc', cc, eye).reshape(n_slab, S5_SLAB_STATE, V7X_LANES)

    w_in = jnp.concatenate([in_mat(bb_re), in_mat(bb_im)], axis=2).astype(BF16)
    w_out = jnp.concatenate([out_mat(c_re.astype(F32)), out_mat(-c_im.astype(F32))],
                            axis=1).astype(BF16)

    flat = lambda t: t.reshape(n_slab, 1, S5_SLAB_STATE)

    def powers(base_re, base_im, n):
        pw_re, pw_im = [base_re], [base_im]
        for _ in range(n - 1):
            r, i = pw_re[-1], pw_im[-1]
            pw_re.append(r * base_re - i * base_im)
            pw_im.append(r * base_im + i * base_re)
        return pw_re, pw_im

    def block_scan_tables(base_re, base_im):
        pw_re, pw_im = powers(base_re, base_im, V7X_SUBLANES)
        rows = jnp.arange(V7X_SUBLANES)[None, :, None]
        tabs = [jnp.concatenate([flat(t) for t in pw_re], axis=1),
                jnp.concatenate([flat(t) for t in pw_im], axis=1)]
        for k in (1, 2, 4):
            tabs.append(jnp.where(rows >= k, flat(pw_re[k - 1]), 0.0))
            tabs.append(jnp.where(rows >= k, flat(pw_im[k - 1]), 0.0))
        return jnp.stack(tabs, axis=1)

    scan = block_scan_tables(ab_re, ab_im)
    q_re, q_im = powers(ab_re, ab_im, seg_len)
    qpow = jnp.stack([jnp.concatenate([flat(t) for t in q_re], axis=1),
                      jnp.concatenate([flat(t) for t in q_im], axis=1)], axis=1)
    seg = block_scan_tables(q_re[-1], q_im[-1])
    dvec = d.astype(F32).reshape(n_slab, 1, V7X_LANES)
    return w_in, w_out, scan, dvec, qpow, seg


def _glu(y, wglu_ref, bglu_ref):
    g = jax.nn.gelu(y)
    z = jnp.dot(g.astype(BF16), wglu_ref[...], preferred_element_type=F32) + bglu_ref[...]
    return g * jax.nn.sigmoid(z)


def _s5_prompt_kernel(u_ref, perm_ref, permt_ref, win_ref, wout_ref, qpow_ref, seg_ref, d_ref,
                      wglu_ref, bglu_ref, o_ref, st_ref, x_scr, y_scr, c_scr, *, tt):
    t = pl.program_id(0)
    n_slab = win_ref.shape[0]
    ns = S5_SLAB_STATE
    sub = V7X_SUBLANES
    seg_len = tt // sub

    @pl.when(t == 0)
    def _():
        c_scr[...] = jnp.zeros(c_scr.shape, F32)

    u = u_ref[...]
    ub = jnp.dot(perm_ref[...], u.astype(BF16), preferred_element_type=F32).astype(BF16)
    row = lax.broadcasted_iota(jnp.int32, (sub, ns), 0)

    for j in range(n_slab):
        lanes = slice(j * V7X_LANES, (j + 1) * V7X_LANES)
        x_scr[...] = jnp.dot(ub[:, lanes], win_ref[j], preferred_element_type=F32)
        a_re, a_im = qpow_ref[j, 0, 0:1, :], qpow_ref[j, 1, 0:1, :]

        xr, xi = x_scr[0:sub, 0:ns], x_scr[0:sub, ns:2 * ns]
        for i in range(1, seg_len):
            rows = slice(i * sub, (i + 1) * sub)
            xr, xi = (a_re * xr - a_im * xi + x_scr[rows, 0:ns],
                      a_re * xi + a_im * xr + x_scr[rows, ns:2 * ns])
            x_scr[rows, 0:ns] = xr
            x_scr[rows, ns:2 * ns] = xi

        cr, ci = c_scr[j:j + 1, 0:ns], c_scr[j:j + 1, ns:2 * ns]
        er, ei = xr, xi
        for n, k in enumerate((1, 2, 4)):
            mr, mi = seg_ref[j, 2 + 2 * n], seg_ref[j, 3 + 2 * n]
            sr, si = pltpu.roll(er, k, 0), pltpu.roll(ei, k, 0)
            er, ei = er + mr * sr - mi * si, ei + mr * si + mi * sr
        p_re, p_im = seg_ref[j, 0], seg_ref[j, 1]
        er, ei = er + p_re * cr - p_im * ci, ei + p_re * ci + p_im * cr
        c_scr[j:j + 1, 0:ns] = er[sub - 1:sub]
        c_scr[j:j + 1, ns:2 * ns] = ei[sub - 1:sub]
        sr = jnp.where(row == 0, cr, pltpu.roll(er, 1, 0))
        si = jnp.where(row == 0, ci, pltpu.roll(ei, 1, 0))

        for i in range(seg_len):
            rows = slice(i * sub, (i + 1) * sub)
            q_re, q_im = qpow_ref[j, 0, i:i + 1, :], qpow_ref[j, 1, i:i + 1, :]
            x_scr[rows, 0:ns] += q_re * sr - q_im * si
            x_scr[rows, ns:2 * ns] += q_re * si + q_im * sr
        y_scr[:, lanes] = jnp.dot(x_scr[...].astype(BF16), wout_ref[j],
                                  preferred_element_type=F32)

    yp = y_scr[...]
    hi = yp.astype(BF16)
    rest = yp - hi.astype(F32)
    mid = rest.astype(BF16)
    low = (rest - mid.astype(F32)).astype(BF16)
    permt = permt_ref[...]
    y = (jnp.dot(permt, hi, preferred_element_type=F32)
         + jnp.dot(permt, mid, preferred_element_type=F32)
         + jnp.dot(permt, low, preferred_element_type=F32))
    y = y + d_ref[...] * u
    o_ref[...] = _glu(y, wglu_ref, bglu_ref).astype(o_ref.dtype)
    st_ref[...] = c_scr[...]


def _s5_prompt(u, tabs, w_glu_b, b_glu, tt):
    s, width = u.shape
    w_in, w_out, _, dvec, qpow, seg = tabs
    n_slab = w_in.shape[0]
    tt = _tile(s, tt)
    assert qpow.shape[2] * V7X_SUBLANES == tt
    seg_len = tt // V7X_SUBLANES
    src = (jnp.arange(tt) % V7X_SUBLANES) * seg_len + jnp.arange(tt) // V7X_SUBLANES
    perm = (src[:, None] == jnp.arange(tt)[None, :]).astype(BF16)
    full = lambda a: pl.BlockSpec(a.shape, lambda t: (0,) * a.ndim)
    bglu = b_glu.reshape(1, width).astype(F32)
    drow = dvec.reshape(1, width)
    kern = functools.partial(_s5_prompt_kernel, tt=tt)
    consts = (perm, perm.T, w_in, w_out, qpow, seg, drow, w_glu_b, bglu)
    out, st = pl.pallas_call(
        kern,
        grid=(s // tt,),
        in_specs=[pl.BlockSpec((tt, width), lambda t: (t, 0))] + [full(a) for a in consts],
        out_specs=[pl.BlockSpec((tt, width), lambda t: (t, 0)),
                   pl.BlockSpec((n_slab, 2 * S5_SLAB_STATE), lambda t: (0, 0))],
        out_shape=[jax.ShapeDtypeStruct((s, width), BF16),
                   jax.ShapeDtypeStruct((n_slab, 2 * S5_SLAB_STATE), F32)],
        scratch_shapes=[pltpu.VMEM((tt, 2 * S5_SLAB_STATE), F32),
                        pltpu.VMEM((tt, width), F32),
                        pltpu.VMEM((n_slab, 2 * S5_SLAB_STATE), F32)],
        compiler_params=_params(("arbitrary",)),
        name="s5_prompt",
    )(u, *consts)
    groups = n_slab * S5_SLAB_GROUPS
    s_re = st[:, :S5_SLAB_STATE].reshape(groups, S5_STATE)
    s_im = st[:, S5_SLAB_STATE:].reshape(groups, S5_STATE)
    return out, s_re, s_im


def _s5_sample_kernel(u_ref, sre_ref, sim_ref, win_ref, wout_ref, scan_ref, d_ref, wglu_ref,
                      bglu_ref, o_ref, xre_ref, xim_ref, y_scr):
    n_slab = win_ref.shape[0]
    ns = S5_SLAB_STATE
    for j in range(n_slab):
        lanes = slice(j * V7X_LANES, (j + 1) * V7X_LANES)
        st = slice(j * ns, (j + 1) * ns)
        u_j = u_ref[:, lanes]
        bu = jnp.dot(u_j.astype(BF16), win_ref[j], preferred_element_type=F32)
        ab_re, ab_im = scan_ref[j, 0, 0:1], scan_ref[j, 1, 0:1]
        s_re, s_im = sre_ref[:, st], sim_ref[:, st]
        xr = ab_re * s_re - ab_im * s_im + bu[:, 0:ns]
        xi = ab_re * s_im + ab_im * s_re + bu[:, ns:2 * ns]
        xre_ref[:, st] = xr
        xim_ref[:, st] = xi
        x = jnp.concatenate([xr, xi], axis=1).astype(BF16)
        y = jnp.dot(x, wout_ref[j], preferred_element_type=F32)
        y_scr[:, lanes] = y + d_ref[j] * u_j
    o_ref[...] = _glu(y_scr[...], wglu_ref, bglu_ref).astype(o_ref.dtype)


def _s5_sample(u, s_re, s_im, tabs, w_glu_b, b_glu):
    db, width = u.shape
    w_in, w_out, scan, dvec, _, _ = tabs
    n_state = s_re.shape[1] * s_re.shape[2]
    bglu = b_glu.reshape(1, width).astype(F32)
    args = (u, s_re.reshape(db, n_state).astype(F32), s_im.reshape(db, n_state).astype(F32),
            w_in, w_out, scan, dvec, w_glu_b, bglu)
    full = lambda a: pl.BlockSpec(a.shape, lambda i: (0,) * a.ndim)
    out, x_re, x_im = pl.pallas_call(
        _s5_sample_kernel,
        grid=(1,),
        in_specs=[full(a) for a in args],
        out_specs=[pl.BlockSpec((db, width), lambda i: (0, 0)),
                   pl.BlockSpec((db, n_state), lambda i: (0, 0)),
                   pl.BlockSpec((db, n_state), lambda i: (0, 0))],
        out_shape=[jax.ShapeDtypeStruct((db, width), BF16),
                   jax.ShapeDtypeStruct((db, n_state), F32),
                   jax.ShapeDtypeStruct((db, n_state), F32)],
        scratch_shapes=[pltpu.VMEM((db, width), F32)],
        compiler_params=_params(("arbitrary",)),
        name="s5_sample",
    )(*args)
    return out, x_re.reshape(s_re.shape), x_im.reshape(s_im.shape)


def _outproj_kernel(x_ref, a_ref, s_ref, w_ref, g_ref, h_ref, f_ref):
    half = a_ref.shape[1]
    h = (x_ref[...]
         + jnp.dot(a_ref[...], w_ref[0:half, :], preferred_element_type=F32)
         + jnp.dot(s_ref[...], w_ref[half:, :], preferred_element_type=F32))
    h_ref[...] = h
    ms = jnp.mean(h * h, axis=-1, keepdims=True)
    f_ref[...] = (h * lax.rsqrt(ms + EPS) * g_ref[...]).astype(BF16)


def _outproj(x, att, s5o, w_out_b, norm_ffn, tm):
    m, d = x.shape
    half = att.shape[1]
    tm = _tile(m, tm)
    row = lambda i: (i, 0)
    const = lambda i: (0, 0)
    return pl.pallas_call(
        _outproj_kernel,
        grid=(m // tm,),
        in_specs=[pl.BlockSpec((tm, d), row), pl.BlockSpec((tm, half), row),
                  pl.BlockSpec((tm, half), row), pl.BlockSpec((d, d), const),
                  pl.BlockSpec((1, d), const)],
        out_specs=[pl.BlockSpec((tm, d), row), pl.BlockSpec((tm, d), row)],
        out_shape=[jax.ShapeDtypeStruct((m, d), F32), jax.ShapeDtypeStruct((m, d), BF16)],
        compiler_params=_params(("parallel",)),
        name="out_proj",
    )(x, att, s5o, w_out_b, norm_ffn.reshape(1, d))


def _ffn_init(h_ref, o_ref):
    @pl.when(pl.program_id(1) == 0)
    def _():
        o_ref[...] = h_ref[...]


def _ffn_down(acts, wd_ref, o_ref):
    tc = acts[0].shape[1]
    part = jnp.dot(acts[0], wd_ref[0:tc, :], preferred_element_type=F32)
    for n in range(1, len(acts)):
        part = part + jnp.dot(acts[n], wd_ref[n * tc:(n + 1) * tc, :],
                              preferred_element_type=F32)
    o_ref[...] += part


def _ffn_prompt_kernel(f_ref, h_ref, wg_ref, wu_ref, wd_ref, cw_ref, cb_ref,
                       o_ref, buf_ref, hgx_ref, prev_ref, *, tm, tc):
    i = pl.program_id(0)
    j = pl.program_id(1)
    halo = V7X_SUBLANES
    tf = wg_ref.shape[1]
    _ffn_init(h_ref, o_ref)

    @pl.when(i == 0)
    def _():
        hgx_ref[0:halo, :] = jnp.zeros((halo, tf), F32)

    @pl.when(i > 0)
    def _():
        hgx_ref[0:halo, :] = prev_ref[j]

    f = f_ref[...]
    blocks = [slice(c0, c0 + tc) for c0 in range(0, tf, tc)]
    hgs = [None] * len(blocks)
    hus = [None] * len(blocks)

    def up(n):
        hgs[n] = jnp.dot(f, wg_ref[:, blocks[n]], preferred_element_type=F32)
        hus[n] = jnp.dot(f, wu_ref[:, blocks[n]], preferred_element_type=F32)

    up(0)
    part = None
    for n, cs in enumerate(blocks):
        if n + 1 < len(blocks):
            up(n + 1)
        hg = hgs[n]
        hgx_ref[halo:, cs] = hg
        cw = cw_ref[:, cs]
        conv = cb_ref[:, cs] + cw[2:3] * hg
        for tap in range(CONV_W - 1):
            off = halo - (CONV_W - 1) + tap
            conv = conv + cw[tap:tap + 1] * hgx_ref[pl.ds(off, tm), cs]
        act = (jax.nn.gelu(conv) * hus[n]).astype(BF16)
        down = jnp.dot(act, wd_ref[cs, :], preferred_element_type=F32)
        part = down if part is None else part + down
    tail = hgx_ref[tm:tm + halo, :]
    prev_ref[j] = tail
    buf_ref[...] = tail
    o_ref[...] += part


def _ffn_sample_kernel(f_ref, h_ref, wg_ref, wu_ref, wd_ref, cw_ref, cb_ref, b0_ref, b1_ref,
                       o_ref, hg_ref):
    _ffn_init(h_ref, o_ref)
    f = f_ref[...]
    hg = jnp.dot(f, wg_ref[...], preferred_element_type=F32)
    hu = jnp.dot(f, wu_ref[...], preferred_element_type=F32)
    hg_ref[...] = hg
    cw = cw_ref[...]
    conv = cb_ref[...] + cw[0:1] * b0_ref[...] + cw[1:2] * b1_ref[...] + cw[2:3] * hg
    _ffn_down([(jax.nn.gelu(conv) * hu).astype(BF16)], wd_ref, o_ref)


def _ffn(f, h, w_gate_b, w_up_b, w_down_b, conv_w, conv_b, conv_buf, tm, tf):
    m, d = f.shape
    dff = w_gate_b.shape[1]
    tm = _tile(m, tm)
    tf = _tile(dff, tf)
    n_ff = dff // tf
    row = lambda i, j: (i, 0)
    col = lambda i, j: (0, j)
    common_in = [pl.BlockSpec((tm, d), row), pl.BlockSpec((tm, d), row),
                 pl.BlockSpec((d, tf), col), pl.BlockSpec((d, tf), col),
                 pl.BlockSpec((tf, d), lambda i, j: (j, 0)),
                 pl.BlockSpec((CONV_W, tf), col), pl.BlockSpec((1, tf), col)]
    common_args = (f, h, w_gate_b, w_up_b, w_down_b, conv_w.astype(F32),
                   conv_b.reshape(1, dff).astype(F32))
    if conv_buf is None:
        halo = V7X_SUBLANES
        kern = functools.partial(_ffn_prompt_kernel, tm=tm, tc=_tile(tf, V7X_MXU_COLS))
        out, tail = pl.pallas_call(
            kern,
            grid=(m // tm, n_ff),
            in_specs=common_in,
            out_specs=[pl.BlockSpec((tm, d), row),
                       pl.BlockSpec((None, halo, tf), lambda i, j: (i, 0, j))],
            out_shape=[jax.ShapeDtypeStruct((m, d), F32),
                       jax.ShapeDtypeStruct((m // tm, halo, dff), F32)],
            scratch_shapes=[pltpu.VMEM((tm + halo, tf), F32),
                            pltpu.VMEM((n_ff, halo, tf), F32)],
            compiler_params=_params(("arbitrary", "arbitrary")),
            name="ffn_prompt",
        )(*common_args)
        return out, tail[-1, halo - (CONV_W - 1):]
    b0 = conv_buf[:, 0, :].astype(F32)
    b1 = conv_buf[:, 1, :].astype(F32)
    out, hg = pl.pallas_call(
        _ffn_sample_kernel,
        grid=(m // tm, n_ff),
        in_specs=common_in + [pl.BlockSpec((tm, tf), lambda i, j: (i, j))] * 2,
        out_specs=[pl.BlockSpec((tm, d), row), pl.BlockSpec((tm, tf), lambda i, j: (i, j))],
        out_shape=[jax.ShapeDtypeStruct((m, d), F32), jax.ShapeDtypeStruct((m, dff), F32)],
        compiler_params=_params(("parallel", "arbitrary")),
        name="ffn_sample",
    )(*common_args, b0, b1)
    return out, jnp.stack([b1, hg], axis=1)


def _ple_kernel(h_ref, p_ref, wg_ref, wp_ref, g_ref, o_ref):
    h = h_ref[...]
    gate = jax.nn.sigmoid(jnp.dot(h.astype(BF16), wg_ref[...], preferred_element_type=F32))
    e = jnp.dot(p_ref[...].astype(BF16), wp_ref[...], preferred_element_type=F32)
    ms = jnp.mean(e * e, axis=-1, keepdims=True)
    o_ref[...] = h + gate * (e * lax.rsqrt(ms + EPS) * g_ref[...])


def _ple(h, p, w_gate_b, w_proj_b, ple_norm, tm):
    m, d = h.shape
    pd = p.shape[1]
    tm = _tile(m, tm)
    row = lambda i: (i, 0)
    const = lambda i: (0, 0)
    return pl.pallas_call(
        _ple_kernel,
        grid=(m // tm,),
        in_specs=[pl.BlockSpec((tm, d), row), pl.BlockSpec((tm, pd), row),
                  pl.BlockSpec((d, d), const), pl.BlockSpec((pd, d), const),
                  pl.BlockSpec((1, d), const)],
        out_specs=pl.BlockSpec((tm, d), row),
        out_shape=jax.ShapeDtypeStruct((m, d), F32),
        compiler_params=_params(("parallel",)),
        name="ple_gate",
    )(h, p, w_gate_b, w_proj_b, ple_norm.reshape(1, d))


def kernel(x_prompt, x_sample, cache_k, cache_v, state_s5_re, state_s5_im, state_conv, page_table, p_prompt, p_sample, norm_mix, w_in, q_norm, k_norm, lam_q1, lam_k1, lam_q2, lam_k2, subln, s5_a_re, s5_a_im, s5_b_re, s5_b_im, s5_c_re, s5_c_im, s5_d, s5_log_dt, w_glu, b_glu, w_out, norm_ffn, w_gate, w_up, conv_w, conv_b, w_down, w_ple_gate, w_ple_proj, ple_norm):
    depth = w_in.shape[0]
    b, s, d = x_prompt.shape
    db, t_new, _ = x_sample.shape
    assert b == 1 and t_new == 1
    page = cache_k.shape[2]
    past_len = page_table.shape[1] * page
    hp = x_prompt.reshape(s, d)
    hs = x_sample.reshape(db, d)
    pos_p = jnp.arange(s)
    pos_s = jnp.full((db,), past_len)
    n_pool = cache_k.shape[1]
    ck = jnp.transpose(cache_k, (0, 1, 3, 4, 5, 2)).reshape(depth * n_pool, -1, page)
    cv = cache_v.reshape(depth * n_pool, page * N_HEADS, V_DIM)
    outs = [[] for _ in range(10)]
    for i in range(depth):
        lam_init = 0.8 - 0.6 * math.exp(-0.3 * i)
        lam_vec = jnp.stack([lam_q1[i], lam_k1[i], lam_q2[i], lam_k2[i]]).astype(F32)
        w_in_b = w_in[i].astype(BF16)
        w_glu_b = w_glu[i].astype(BF16)
        w_out_b = w_out[i].astype(BF16)
        w_gate_b = w_gate[i].astype(BF16)
        w_up_b = w_up[i].astype(BF16)
        w_down_b = w_down[i].astype(BF16)
        w_pg_b = w_ple_gate[i].astype(BF16)
        w_pp_b = w_ple_proj[i].astype(BF16)
        tabs = _s5_tables(s5_a_re[i], s5_a_im[i], s5_b_re[i], s5_b_im[i], s5_c_re[i],
                          s5_c_im[i], s5_d[i], s5_log_dt[i], S5_TILE // V7X_SUBLANES)

        q, k32, kb, v32, vt, u = _project(hp, pos_p, norm_mix[i], w_in_b, q_norm[i], k_norm[i],
                                          512, True)
        qs, k32s, kbs, v32s, vbs, us = _project(hs, pos_s, norm_mix[i], w_in_b, q_norm[i],
                                                k_norm[i], 512, False)
        att, att_s = _attention(q, kb, vt, qs, kbs, vbs, ck, cv, page_table + i * n_pool,
                                lam_vec, subln[i], lam_init, 1024)

        s5o, s_re, s_im = _s5_prompt(u, tabs, w_glu_b, b_glu[i], S5_TILE)
        h1, f = _outproj(hp, att, s5o, w_out_b, norm_ffn[i], 512)
        h2, cbuf = _ffn(f, h1, w_gate_b, w_up_b, w_down_b, conv_w[i], conv_b[i], None, 512, 512)
        hp = _ple(h2, p_prompt[i].reshape(s, -1), w_pg_b, w_pp_b, ple_norm[i], 512)
        outs[0].append(k32.reshape(b, s, N_HEADS, 2, HEAD_DIM))
        outs[1].append(v32.reshape(b, s, N_HEADS, V_DIM))
        outs[2].append(s_re[None])
        outs[3].append(s_im[None])
        outs[4].append(cbuf[None])

        s5o, s_re, s_im = _s5_sample(us, state_s5_re[i], state_s5_im[i], tabs, w_glu_b, b_glu[i])
        h1, f = _outproj(hs, att_s, s5o, w_out_b, norm_ffn[i], 512)
        h2, cbuf = _ffn(f, h1, w_gate_b, w_up_b, w_down_b, conv_w[i], conv_b[i], state_conv[i],
                        512, 512)
        hs = _ple(h2, p_sample[i].reshape(db, -1), w_pg_b, w_pp_b, ple_norm[i], 512)
        outs[5].append(k32s.reshape(db, t_new, N_HEADS, 2, HEAD_DIM))
        outs[6].append(v32s.reshape(db, t_new, N_HEADS, V_DIM))
        outs[7].append(s_re)
        outs[8].append(s_im)
        outs[9].append(cbuf)
    return (hp.reshape(b, s, d), hs.reshape(db, t_new, d)) + tuple(jnp.stack(o) for o in outs)
```

```python
import functools
import math

import jax
import jax.numpy as jnp
import numpy as np
from jax import lax
from jax.experimental import pallas as pl
from jax.experimental.pallas import tpu as pltpu

N_HEADS = 8
HEAD_DIM = 64
V_DIM = 2 * HEAD_DIM
ROT_DIM = HEAD_DIM // 4
ROPE_THETA = 500000.0
S5_CH = 16
S5_STATE = 64
CONV_W = 3
EPS = 1e-6
Q_SCALE = math.log2(math.e) * HEAD_DIM ** -0.5

V7X_LANES = 128
V7X_SUBLANES = 8
V7X_MXU_COLS = 256
VMEM_LIMIT_BYTES = 56 * 1024 * 1024
ATTN_VMEM_LIMIT_BYTES = 60 * 1024 * 1024

S5_SLAB_GROUPS = V7X_LANES // S5_CH
S5_SLAB_STATE = S5_SLAB_GROUPS * S5_STATE
S5_TILE = 256

F32 = jnp.float32
BF16 = jnp.bfloat16


def _tile(n, pref):
    t = min(n, pref)
    assert n % t == 0, (n, t)
    return t


def _params(sem, vmem=VMEM_LIMIT_BYTES):
    return pltpu.CompilerParams(dimension_semantics=sem, vmem_limit_bytes=vmem)


def _qk_post(p, gain, cos, sin_lo, sin_hi, seg):
    outs = []
    for c in range(p.shape[1] // V7X_LANES):
        pc = p[:, c * V7X_LANES:(c + 1) * V7X_LANES]
        ms = jnp.dot((pc * pc).astype(BF16), seg, preferred_element_type=F32)
        y = pc * lax.rsqrt(ms + EPS) * gain
        y = (y * cos
             + pltpu.roll(y, V7X_LANES - ROT_DIM // 2, 1) * sin_lo
             + pltpu.roll(y, ROT_DIM // 2, 1) * sin_hi)
        outs.append(y)
    return jnp.concatenate(outs, axis=1)


def _proj_kernel(x_ref, g_ref, w_ref, qg_ref, kg_ref, cos_ref, slo_ref, shi_ref, seg_ref,
                 q_ref, k32_ref, kb_ref, v32_ref, vb_ref, u_ref, *, v_transposed, bw):
    width = q_ref.shape[1]
    nb = width // bw
    x = x_ref[...]
    ms = jnp.mean(x * x, axis=-1, keepdims=True)
    xn = (x * lax.rsqrt(ms + EPS) * g_ref[...]).astype(BF16)
    rope = (cos_ref[...], slo_ref[...], shi_ref[...], seg_ref[...])

    def block(c):
        return jnp.dot(xn, w_ref[:, c * bw:(c + 1) * bw], preferred_element_type=F32)

    for b in range(nb):
        cols = slice(b * bw, (b + 1) * bw)
        q = _qk_post(block(b), qg_ref[...], *rope)
        q_ref[:, cols] = (q * Q_SCALE).astype(BF16)
    for b in range(nb):
        cols = slice(b * bw, (b + 1) * bw)
        k = _qk_post(block(nb + b), kg_ref[...], *rope)
        k32_ref[:, cols] = k
        kb_ref[:, cols] = k.astype(BF16)
    for b in range(nb):
        cols = slice(b * bw, (b + 1) * bw)
        v = block(2 * nb + b)
        v32_ref[:, cols] = v
        if v_transposed:
            for hh in range(bw // V_DIM):
                vb_ref[b * (bw // V_DIM) + hh, 0] = (
                    v[:, hh * V_DIM:(hh + 1) * V_DIM].T.astype(BF16))
        else:
            vb_ref[:, cols] = v.astype(BF16)
    for b in range(nb):
        u_ref[:, b * bw:(b + 1) * bw] = block(3 * nb + b)


def _rope_tables(pos):
    half = ROT_DIM // 2
    inv = ROPE_THETA ** (-(np.arange(half, dtype=np.float64) * 2.0 / ROT_DIM))
    ang = np.asarray(pos, np.float64)[:, None] * inv[None, :]
    cos, sin = np.cos(ang), np.sin(ang)
    t = ang.shape[0]
    ones = np.ones((t, HEAD_DIM - ROT_DIM))
    zeros = np.zeros((t, HEAD_DIM - ROT_DIM))
    zh = np.zeros((t, half))
    cos_t = np.concatenate([cos, cos, ones], axis=1)
    lo_t = np.concatenate([-sin, zh, zeros], axis=1)
    hi_t = np.concatenate([zh, sin, zeros], axis=1)
    rep = V7X_LANES // HEAD_DIM
    return tuple(jnp.asarray(np.tile(tab, (1, rep)), F32) for tab in (cos_t, lo_t, hi_t))


def _project(x, pos, norm_mix, w_in_b, q_norm, k_norm, tm, v_transposed):
    m, d = x.shape
    width = w_in_b.shape[1] // 4
    tm = _tile(m, tm)
    if v_transposed:
        vb_spec = pl.BlockSpec((N_HEADS, 1, V_DIM, tm), lambda i: (0, i, 0, 0))
        vb_shape = jax.ShapeDtypeStruct((N_HEADS, m // tm, V_DIM, tm), BF16)
    else:
        vb_spec = pl.BlockSpec((tm, width), lambda i: (i, 0))
        vb_shape = jax.ShapeDtypeStruct((m, width), BF16)
    cos_t, lo_t, hi_t = _rope_tables(pos)
    rep = V7X_LANES // HEAD_DIM
    qg = jnp.tile(q_norm.reshape(1, HEAD_DIM), (1, rep))
    kg = jnp.tile(k_norm.reshape(1, HEAD_DIM), (1, rep))
    lane = np.arange(V7X_LANES)
    seg = jnp.asarray((lane[:, None] // HEAD_DIM == lane[None, :] // HEAD_DIM) / HEAD_DIM, BF16)
    row = lambda i: (i, 0)
    const = lambda i: (0, 0)
    out_spec = pl.BlockSpec((tm, width), row)
    kern = functools.partial(_proj_kernel, v_transposed=v_transposed,
                             bw=_tile(width, 2 * V7X_MXU_COLS))
    return pl.pallas_call(
        kern,
        grid=(m // tm,),
        in_specs=[
            pl.BlockSpec((tm, d), row),
            pl.BlockSpec((1, d), const),
            pl.BlockSpec((d, 4 * width), const, pipeline_mode=pl.Buffered(1)),
            pl.BlockSpec((1, V7X_LANES), const),
            pl.BlockSpec((1, V7X_LANES), const),
            pl.BlockSpec((tm, V7X_LANES), row),
            pl.BlockSpec((tm, V7X_LANES), row),
            pl.BlockSpec((tm, V7X_LANES), row),
            pl.BlockSpec((V7X_LANES, V7X_LANES), const),
        ],
        out_specs=[out_spec] * 4 + [vb_spec, out_spec],
        out_shape=[
            jax.ShapeDtypeStruct((m, width), BF16),
            jax.ShapeDtypeStruct((m, width), F32),
            jax.ShapeDtypeStruct((m, width), BF16),
            jax.ShapeDtypeStruct((m, width), F32),
            vb_shape,
            jax.ShapeDtypeStruct((m, width), F32),
        ],
        compiler_params=_params(("parallel",)),
        name="in_proj",
    )(x, norm_mix.reshape(1, d), w_in_b, qg, kg, cos_t, lo_t, hi_t, seg)


def _lambda(lam_ref, lam_init):
    lv = lam_ref[...]
    l1 = jnp.sum(lv[0:1] * lv[1:2], axis=-1, keepdims=True)
    l2 = jnp.sum(lv[2:3] * lv[3:4], axis=-1, keepdims=True)
    return jnp.exp(l1) - jnp.exp(l2) + lam_init


def _subln(out, sub, lam_init):
    ms = jnp.mean(out * out, axis=-1, keepdims=True)
    return out * lax.rsqrt(ms + EPS) * sub * (1.0 - lam_init)


def _prompt_chunk(q_ref, k_ref, vt_ref, acc_ref, st_ref, ml_ref, qi, kc, *, tq, masked):
    tv = vt_ref.shape[-1]
    n_sub = tq // tv
    q = q_ref[...].astype(F32)
    lane = lax.broadcasted_iota(jnp.int32, q.shape, 1)
    qs = jnp.concatenate([jnp.where(lane < HEAD_DIM, q, 0.0),
                          jnp.where(lane >= HEAD_DIM, q, 0.0)], axis=0).astype(BF16)

    def scores(j):
        k0 = pl.multiple_of(kc * tq + j * tv, tv)
        st_ref[j] = lax.dot_general(k_ref[pl.ds(k0, tv), :], qs, (((1,), (1,)), ((), ())),
                                    preferred_element_type=F32)

    m, l = ml_ref[0:1, :], ml_ref[1:2, :]
    scores(0)
    for j in range(n_sub):
        if j + 1 < n_sub:
            scores(j + 1)
        st = st_ref[j]
        if masked:
            key = lax.broadcasted_iota(jnp.int32, st.shape, 0) + j * tv
            qry = lax.broadcasted_iota(jnp.int32, st.shape, 1)
            qry = jnp.where(qry >= tq, qry - tq, qry)
            st = jnp.where(key <= qry, st, -jnp.inf)
        m_new = jnp.maximum(m, jnp.max(st, axis=0, keepdims=True))
        alpha = jnp.exp2(m - m_new)
        p = jnp.exp2(st - m_new)
        l = alpha * l + jnp.sum(p, axis=0, keepdims=True)
        m = m_new
        pv = jnp.dot(vt_ref[kc * n_sub + j], p.astype(BF16), preferred_element_type=F32)
        acc_ref[...] = alpha * acc_ref[...] + pv
    ml_ref[0:1, :] = m
    ml_ref[1:2, :] = l


def _prompt_finish(acc_ref, ml_ref, lam_ref, sub_ref, o_ref, *, tq, lam_init):
    lam = _lambda(lam_ref, lam_init)
    o = acc_ref[...] / ml_ref[1:2, :]
    out = o[:, :tq] - lam * o[:, tq:]
    ms = jnp.mean(out * out, axis=0, keepdims=True)
    y = out * lax.rsqrt(ms + EPS) * sub_ref[...] * (1.0 - lam_init)
    o_ref[...] = y.T.astype(o_ref.dtype)


def _sample_init(q_ref, qm_ref, m_ref, l_ref, acc_ref):
    n_rows, width = qm_ref.shape
    q = jnp.broadcast_to(q_ref[0].astype(F32), (n_rows, width))
    r = lax.broadcasted_iota(jnp.int32, (n_rows, width), 0)
    c = lax.broadcasted_iota(jnp.int32, (n_rows, width), 1)
    keep = (c >= r * HEAD_DIM) & (c < (r + 1) * HEAD_DIM)
    qm_ref[...] = jnp.where(keep, q, 0.0).astype(BF16)
    m_ref[...] = jnp.full(m_ref.shape, -jnp.inf, F32)
    l_ref[...] = jnp.zeros(l_ref.shape, F32)
    acc_ref[...] = jnp.zeros(acc_ref.shape, F32)


def _sample_pages(k_refs, v_refs, exp_ref, qm_ref, m_ref, l_ref, acc_ref, valid):
    n_pg = len(k_refs)
    n_rows = 2 * N_HEADS
    page = k_refs[0].shape[1]
    qm = qm_ref[...]
    s = jnp.concatenate(
        [jnp.dot(qm, k_refs[g][...].astype(BF16), preferred_element_type=F32)
         for g in range(n_pg)], axis=1)
    m_prev = m_ref[...]
    m_new = jnp.maximum(m_prev, jnp.max(s, axis=1, keepdims=True))
    alpha = jnp.exp2(m_prev - m_new)
    p = jnp.exp2(s - m_new)
    l_new = alpha * l_ref[...] + jnp.sum(p, axis=1, keepdims=True)
    pb = p.astype(BF16)
    p_rows = jnp.concatenate([pb[:, g * page:(g + 1) * page] for g in range(n_pg)], axis=0)
    p_wide = jnp.dot(p_rows, exp_ref[...], preferred_element_type=F32)
    r = lax.broadcasted_iota(jnp.int32, p_wide.shape, 0)
    c = lax.broadcasted_iota(jnp.int32, p_wide.shape, 1)
    own = (c % N_HEADS) == ((r % n_rows) // 2)
    p_wide = jnp.where(own, p_wide, 0.0).astype(BF16)
    acc = alpha * acc_ref[...]
    for g in range(n_pg):
        acc = acc + jnp.dot(p_wide[g * n_rows:(g + 1) * n_rows], v_refs[g][...].astype(BF16),
                            preferred_element_type=F32)
    m_ref[...] = jnp.where(valid, m_new, m_prev)
    l_ref[...] = jnp.where(valid, l_new, l_ref[...])
    acc_ref[...] = jnp.where(valid, acc, acc_ref[...])


def _sample_finish(kn_ref, vn_ref, lam_ref, sub_ref, o_ref, qm_ref, m_ref, l_ref, acc_ref,
                   *, lam_init):
    kn = kn_ref[0].astype(F32)
    vn = vn_ref[0].astype(F32)
    s_new = jnp.sum(qm_ref[...].astype(F32) * kn, axis=1, keepdims=True)
    m_old = m_ref[...]
    m_fin = jnp.maximum(m_old, s_new)
    a = jnp.exp2(m_old - m_fin)
    p_new = jnp.exp2(s_new - m_fin)
    l_fin = a * l_ref[...] + p_new
    acc_fin = a * acc_ref[...] + p_new.astype(BF16).astype(F32) * vn
    o = acc_fin / l_fin
    o1 = jnp.concatenate([o[2 * h:2 * h + 1] for h in range(N_HEADS)], axis=0)
    o2 = jnp.concatenate([o[2 * h + 1:2 * h + 2] for h in range(N_HEADS)], axis=0)
    lam = _lambda(lam_ref, lam_init)
    o_ref[0] = _subln(o1 - lam * o2, sub_ref[...], lam_init)


def _attn_kernel(sh_ref, sq_ref, sk_ref, pt_ref, *refs, n_pg, tq, lam_init, n_sample_steps,
                 steps_per_seq):
    del sh_ref, pt_ref
    k_pages = refs[:n_pg]
    v_pages = refs[n_pg:2 * n_pg]
    (q_ref, k_ref, vt_ref, lam_ref, subc_ref, qs_ref, kn_ref, vn_ref, exp_ref, subr_ref,
     o_ref, os_ref, acc_ref, st_ref, ml_ref, qm_ref, ms_ref, ls_ref, accs_ref) = refs[2 * n_pg:]
    t = pl.program_id(0)
    qi = sq_ref[t]
    kc = sk_ref[t]
    valid = t < n_sample_steps
    s_id = t % steps_per_seq

    @pl.when(kc == 0)
    def _():
        acc_ref[...] = jnp.zeros(acc_ref.shape, F32)
        ml_ref[0:1, :] = jnp.full((1, ml_ref.shape[1]), -jnp.inf, F32)
        ml_ref[1:2, :] = jnp.zeros((1, ml_ref.shape[1]), F32)

    @pl.when(valid & (s_id == 0))
    def _():
        _sample_init(qs_ref, qm_ref, ms_ref, ls_ref, accs_ref)

    chunk = functools.partial(_prompt_chunk, q_ref, k_ref, vt_ref, acc_ref, st_ref, ml_ref,
                              qi, kc, tq=tq)
    pages = functools.partial(_sample_pages, k_pages, v_pages, exp_ref, qm_ref, ms_ref, ls_ref,
                              accs_ref, valid)

    @pl.when(kc < qi)
    def _():
        chunk(masked=False)
        pages()

    @pl.when(kc == qi)
    def _():
        chunk(masked=True)
        pages()
        _prompt_finish(acc_ref, ml_ref, lam_ref, subc_ref, o_ref, tq=tq, lam_init=lam_init)

    @pl.when(valid & (s_id == steps_per_seq - 1))
    def _():
        _sample_finish(kn_ref, vn_ref, lam_ref, subr_ref, os_ref, qm_ref, ms_ref, ls_ref,
                       accs_ref, lam_init=lam_init)


def _attention(q, k, vt, qs, k_new, v_new, ck, cv, page_idx, lam_vec, subln, lam_init, tq):
    s, width = q.shape
    db = qs.shape[0]
    tv = vt.shape[-1]
    tq = _tile(s, max(tq, tv))
    assert tq % tv == 0
    nq = s // tq
    page = ck.shape[2]
    n_pages = page_idx.shape[1]
    n_rows = 2 * N_HEADS

    sched = [(h, qi, kc) for h in range(N_HEADS) for qi in range(nq) for kc in range(qi + 1)]
    n_steps = len(sched)
    sh, sq, sk = (jnp.asarray([e[n] for e in sched], jnp.int32) for n in range(3))
    n_pg = next(g for g in range(1, n_pages + 1)
                if n_pages % g == 0 and db * (n_pages // g) <= n_steps)
    steps_per_seq = n_pages // n_pg
    n_sample_steps = db * steps_per_seq

    tok = np.arange(page)
    expand = jnp.asarray(tok[:, None] == (np.arange(page * N_HEADS)[None, :] // N_HEADS), BF16)
    vn = jnp.repeat(v_new.reshape(db, N_HEADS, V_DIM), 2, axis=1)

    def seq_of(t):
        ts = jnp.minimum(t, n_sample_steps - 1)
        return ts // steps_per_seq, ts % steps_per_seq

    def page_spec(shape, g):
        def index(t, sh, sq, sk, pt):
            b, s_id = seq_of(t)
            return (pt[b, s_id * n_pg + g], 0, 0)
        return pl.BlockSpec((None,) + shape, index)

    const = lambda t, sh, sq, sk, pt: (0, 0)
    tile = lambda t, sh, sq, sk, pt: (sq[t], sh[t])
    seq = lambda t, sh, sq, sk, pt: (seq_of(t)[0], 0, 0)
    kern = functools.partial(_attn_kernel, n_pg=n_pg, tq=tq, lam_init=lam_init,
                             n_sample_steps=n_sample_steps, steps_per_seq=steps_per_seq)
    att, att_s = pl.pallas_call(
        kern,
        grid_spec=pltpu.PrefetchScalarGridSpec(
            num_scalar_prefetch=4,
            grid=(n_steps,),
            in_specs=([page_spec((width, page), g) for g in range(n_pg)]
                      + [page_spec((page * N_HEADS, V_DIM), g) for g in range(n_pg)]
                      + [pl.BlockSpec((tq, V_DIM), tile),
                         pl.BlockSpec((s, V_DIM), lambda t, sh, sq, sk, pt: (0, sh[t])),
                         pl.BlockSpec((None, s // tv, V_DIM, tv),
                                      lambda t, sh, sq, sk, pt: (sh[t], 0, 0, 0)),
                         pl.BlockSpec((4, HEAD_DIM), const),
                         pl.BlockSpec((V_DIM, 1), const),
                         pl.BlockSpec((1, 1, width), seq), pl.BlockSpec((1, 1, width), seq),
                         pl.BlockSpec((1, n_rows, V_DIM), seq),
                         pl.BlockSpec(expand.shape, const),
                         pl.BlockSpec((1, V_DIM), const)]),
            out_specs=[pl.BlockSpec((tq, V_DIM), tile),
                       pl.BlockSpec((1, N_HEADS, V_DIM), seq)],
            scratch_shapes=[pltpu.VMEM((V_DIM, 2 * tq), F32),
                            pltpu.VMEM((tq // tv, tv, 2 * tq), F32),
                            pltpu.VMEM((2, 2 * tq), F32),
                            pltpu.VMEM((n_rows, width), BF16),
                            pltpu.VMEM((n_rows, 1), F32), pltpu.VMEM((n_rows, 1), F32),
                            pltpu.VMEM((n_rows, V_DIM), F32)],
        ),
        out_shape=[jax.ShapeDtypeStruct((s, width), BF16),
                   jax.ShapeDtypeStruct((db, N_HEADS, V_DIM), F32)],
        compiler_params=_params(("arbitrary",), ATTN_VMEM_LIMIT_BYTES),
        name="attention",
    )(sh, sq, sk, page_idx, *([ck] * n_pg), *([cv] * n_pg),
      q, k, vt, lam_vec, subln.reshape(V_DIM, 1),
      qs.reshape(db, 1, width), k_new.reshape(db, 1, width), vn, expand,
      subln.reshape(1, V_DIM))
    return att, att_s.reshape(db, width).astype(BF16)


def _s5_tables(a_re, a_im, b_re, b_im, c_re, c_im, d, log_dt, seg_len):
    g, p = a_re.shape
    n_slab = g // S5_SLAB_GROUPS
    dt = jnp.exp(log_dt.astype(F32))[:, None]
    mag = jnp.exp(dt * a_re)
    ab_re = mag * jnp.cos(dt * a_im)
    ab_im = mag * jnp.sin(dt * a_im)
    den = a_re * a_re + a_im * a_im
    n_re = ab_re - 1.0
    f_re = (n_re * a_re + ab_im * a_im) / den
    f_im = (ab_im * a_re - n_re * a_im) / den
    bb_re = f_re[..., None] * b_re - f_im[..., None] * b_im
    bb_im = f_re[..., None] * b_im + f_im[..., None] * b_re
    eye = jnp.eye(S5_SLAB_GROUPS, dtype=F32)

    def in_mat(bb):
        bb = bb.reshape(n_slab, S5_SLAB_GROUPS, p, S5_CH)
        return jnp.einsum('jgpc,gh->jgchp', bb, eye).reshape(n_slab, V7X_LANES, S5_SLAB_STATE)

    def out_mat(cc):
        cc = cc.reshape(n_slab, S5_SLAB_GROUPS, S5_CH, p)
        return jnp.einsum('jgcp,gh->jgphc', cc, eye).reshape(n_slab, S5_SLAB_STATE, V7X_LANES)

    w_in = jnp.concatenate([in_mat(bb_re), in_mat(bb_im)], axis=2).astype(BF16)
    w_out = jnp.concatenate([out_mat(c_re.astype(F32)), out_mat(-c_im.astype(F32))],
                            axis=1).astype(BF16)

    def apow(ns):
        ns = jnp.asarray(ns, F32)[:, None, None]
        mag = jnp.exp(ns * (dt * a_re))
        ang = ns * (dt * a_im)
        flat = lambda t: jnp.swapaxes(t.reshape(-1, n_slab, S5_SLAB_STATE), 0, 1)
        return flat(mag * jnp.cos(ang)), flat(mag * jnp.sin(ang))

    def block_scan_tables(step):
        p_re, p_im = apow(step * np.arange(1, V7X_SUBLANES + 1))
        rows = np.arange(V7X_SUBLANES)[None, :, None]
        tabs = [p_re, p_im]
        for k in (1, 2, 4):
            tabs.append(jnp.where(rows >= k, p_re[:, k - 1:k], 0.0))
            tabs.append(jnp.where(rows >= k, p_im[:, k - 1:k], 0.0))
        return jnp.stack(tabs, axis=1)

    scan = block_scan_tables(1)
    qpow = jnp.stack(apow(np.arange(1, seg_len + 1)), axis=1)
    seg = block_scan_tables(seg_len)
    dvec = d.astype(F32).reshape(n_slab, 1, V7X_LANES)
    return w_in, w_out, scan, dvec, qpow, seg


def _glu(y, wglu_ref, bglu_ref):
    g = jax.nn.gelu(y)
    z = jnp.dot(g.astype(BF16), wglu_ref[...], preferred_element_type=F32) + bglu_ref[...]
    return g * jax.nn.sigmoid(z)


def _s5_prompt_kernel(u_ref, perm_ref, permt_ref, win_ref, wout_ref, qpow_ref, seg_ref, d_ref,
                      wglu_ref, bglu_ref, o_ref, st_ref, x_scr, y_scr, c_scr, *, tt):
    t = pl.program_id(0)
    n_slab = win_ref.shape[0]
    ns = S5_SLAB_STATE
    sub = V7X_SUBLANES
    seg_len = tt // sub

    @pl.when(t == 0)
    def _():
        c_scr[...] = jnp.zeros(c_scr.shape, F32)

    u = u_ref[...]
    ub = jnp.dot(perm_ref[...], u.astype(BF16), preferred_element_type=F32).astype(BF16)
    row = lax.broadcasted_iota(jnp.int32, (sub, ns), 0)

    for j in range(n_slab):
        lanes = slice(j * V7X_LANES, (j + 1) * V7X_LANES)
        x_scr[...] = jnp.dot(ub[:, lanes], win_ref[j], preferred_element_type=F32)
        a_re, a_im = qpow_ref[j, 0, 0:1, :], qpow_ref[j, 1, 0:1, :]

        xr, xi = x_scr[0:sub, 0:ns], x_scr[0:sub, ns:2 * ns]
        for i in range(1, seg_len):
            rows = slice(i * sub, (i + 1) * sub)
            xr, xi = (a_re * xr - a_im * xi + x_scr[rows, 0:ns],
                      a_re * xi + a_im * xr + x_scr[rows, ns:2 * ns])
            x_scr[rows, 0:ns] = xr
            x_scr[rows, ns:2 * ns] = xi

        cr, ci = c_scr[j:j + 1, 0:ns], c_scr[j:j + 1, ns:2 * ns]
        er, ei = xr, xi
        for n, k in enumerate((1, 2, 4)):
            mr, mi = seg_ref[j, 2 + 2 * n], seg_ref[j, 3 + 2 * n]
            sr, si = pltpu.roll(er, k, 0), pltpu.roll(ei, k, 0)
            er, ei = er + mr * sr - mi * si, ei + mr * si + mi * sr
        p_re, p_im = seg_ref[j, 0], seg_ref[j, 1]
        er, ei = er + p_re * cr - p_im * ci, ei + p_re * ci + p_im * cr
        c_scr[j:j + 1, 0:ns] = er[sub - 1:sub]
        c_scr[j:j + 1, ns:2 * ns] = ei[sub - 1:sub]
        sr = jnp.where(row == 0, cr, pltpu.roll(er, 1, 0))
        si = jnp.where(row == 0, ci, pltpu.roll(ei, 1, 0))

        for i in range(seg_len):
            rows = slice(i * sub, (i + 1) * sub)
            q_re, q_im = qpow_ref[j, 0, i:i + 1, :], qpow_ref[j, 1, i:i + 1, :]
            x_scr[rows, 0:ns] += q_re * sr - q_im * si
            x_scr[rows, ns:2 * ns] += q_re * si + q_im * sr
        y_scr[:, lanes] = jnp.dot(x_scr[...].astype(BF16), wout_ref[j],
                                  preferred_element_type=F32)

    yp = y_scr[...]
    hi = yp.astype(BF16)
    rest = yp - hi.astype(F32)
    mid = rest.astype(BF16)
    low = (rest - mid.astype(F32)).astype(BF16)
    permt = permt_ref[...]
    y = (jnp.dot(permt, hi, preferred_element_type=F32)
         + jnp.dot(permt, mid, preferred_element_type=F32)
         + jnp.dot(permt, low, preferred_element_type=F32))
    y = y + d_ref[...] * u
    o_ref[...] = _glu(y, wglu_ref, bglu_ref).astype(o_ref.dtype)
    st_ref[...] = c_scr[...]


def _s5_prompt(u, tabs, w_glu_b, b_glu, tt):
    s, width = u.shape
    w_in, w_out, _, dvec, qpow, seg = tabs
    n_slab = w_in.shape[0]
    tt = _tile(s, tt)
    assert qpow.shape[2] * V7X_SUBLANES == tt
    seg_len = tt // V7X_SUBLANES
    src = (np.arange(tt) % V7X_SUBLANES) * seg_len + np.arange(tt) // V7X_SUBLANES
    perm_np = src[:, None] == np.arange(tt)[None, :]
    perm, perm_t = jnp.asarray(perm_np, BF16), jnp.asarray(perm_np.T, BF16)
    full = lambda a: pl.BlockSpec(a.shape, lambda t: (0,) * a.ndim)
    bglu = b_glu.reshape(1, width).astype(F32)
    drow = dvec.reshape(1, width)
    kern = functools.partial(_s5_prompt_kernel, tt=tt)
    consts = (perm, perm_t, w_in, w_out, qpow, seg, drow, w_glu_b, bglu)
    out, st = pl.pallas_call(
        kern,
        grid=(s // tt,),
        in_specs=[pl.BlockSpec((tt, width), lambda t: (t, 0))] + [full(a) for a in consts],
        out_specs=[pl.BlockSpec((tt, width), lambda t: (t, 0)),
                   pl.BlockSpec((n_slab, 2 * S5_SLAB_STATE), lambda t: (0, 0))],
        out_shape=[jax.ShapeDtypeStruct((s, width), BF16),
                   jax.ShapeDtypeStruct((n_slab, 2 * S5_SLAB_STATE), F32)],
        scratch_shapes=[pltpu.VMEM((tt, 2 * S5_SLAB_STATE), F32),
                        pltpu.VMEM((tt, width), F32),
                        pltpu.VMEM((n_slab, 2 * S5_SLAB_STATE), F32)],
        compiler_params=_params(("arbitrary",)),
        name="s5_prompt",
    )(u, *consts)
    groups = n_slab * S5_SLAB_GROUPS
    s_re = st[:, :S5_SLAB_STATE].reshape(groups, S5_STATE)
    s_im = st[:, S5_SLAB_STATE:].reshape(groups, S5_STATE)
    return out, s_re, s_im


def _s5_sample_kernel(u_ref, sre_ref, sim_ref, win_ref, wout_ref, scan_ref, d_ref, wglu_ref,
                      bglu_ref, o_ref, xre_ref, xim_ref, y_scr):
    n_slab = win_ref.shape[0]
    ns = S5_SLAB_STATE
    for j in range(n_slab):
        lanes = slice(j * V7X_LANES, (j + 1) * V7X_LANES)
        st = slice(j * ns, (j + 1) * ns)
        u_j = u_ref[:, lanes]
        bu = jnp.dot(u_j.astype(BF16), win_ref[j], preferred_element_type=F32)
        ab_re, ab_im = scan_ref[j, 0, 0:1], scan_ref[j, 1, 0:1]
        s_re, s_im = sre_ref[:, st], sim_ref[:, st]
        xr = ab_re * s_re - ab_im * s_im + bu[:, 0:ns]
        xi = ab_re * s_im + ab_im * s_re + bu[:, ns:2 * ns]
        xre_ref[:, st] = xr
        xim_ref[:, st] = xi
        x = jnp.concatenate([xr, xi], axis=1).astype(BF16)
        y = jnp.dot(x, wout_ref[j], preferred_element_type=F32)
        y_scr[:, lanes] = y + d_ref[j] * u_j
    o_ref[...] = _glu(y_scr[...], wglu_ref, bglu_ref).astype(o_ref.dtype)


def _s5_sample(u, s_re, s_im, tabs, w_glu_b, b_glu):
    db, width = u.shape
    w_in, w_out, scan, dvec, _, _ = tabs
    n_state = s_re.shape[1] * s_re.shape[2]
    bglu = b_glu.reshape(1, width).astype(F32)
    args = (u, s_re.reshape(db, n_state).astype(F32), s_im.reshape(db, n_state).astype(F32),
            w_in, w_out, scan, dvec, w_glu_b, bglu)
    full = lambda a: pl.BlockSpec(a.shape, lambda i: (0,) * a.ndim)
    out, x_re, x_im = pl.pallas_call(
        _s5_sample_kernel,
        grid=(1,),
        in_specs=[full(a) for a in args],
        out_specs=[pl.BlockSpec((db, width), lambda i: (0, 0)),
                   pl.BlockSpec((db, n_state), lambda i: (0, 0)),
                   pl.BlockSpec((db, n_state), lambda i: (0, 0))],
        out_shape=[jax.ShapeDtypeStruct((db, width), BF16),
                   jax.ShapeDtypeStruct((db, n_state), F32),
                   jax.ShapeDtypeStruct((db, n_state), F32)],
        scratch_shapes=[pltpu.VMEM((db, width), F32)],
        compiler_params=_params(("arbitrary",)),
        name="s5_sample",
    )(*args)
    return out, x_re.reshape(s_re.shape), x_im.reshape(s_im.shape)


def _outproj_kernel(x_ref, a_ref, s_ref, w_ref, g_ref, h_ref, f_ref):
    half = a_ref.shape[1]
    h = (x_ref[...]
         + jnp.dot(a_ref[...], w_ref[0:half, :], preferred_element_type=F32)
         + jnp.dot(s_ref[...], w_ref[half:, :], preferred_element_type=F32))
    h_ref[...] = h
    ms = jnp.mean(h * h, axis=-1, keepdims=True)
    f_ref[...] = (h * lax.rsqrt(ms + EPS) * g_ref[...]).astype(BF16)


def _outproj(x, att, s5o, w_out_b, norm_ffn, tm):
    m, d = x.shape
    half = att.shape[1]
    tm = _tile(m, tm)
    row = lambda i: (i, 0)
    const = lambda i: (0, 0)
    return pl.pallas_call(
        _outproj_kernel,
        grid=(m // tm,),
        in_specs=[pl.BlockSpec((tm, d), row), pl.BlockSpec((tm, half), row),
                  pl.BlockSpec((tm, half), row), pl.BlockSpec((d, d), const),
                  pl.BlockSpec((1, d), const)],
        out_specs=[pl.BlockSpec((tm, d), row), pl.BlockSpec((tm, d), row)],
        out_shape=[jax.ShapeDtypeStruct((m, d), F32), jax.ShapeDtypeStruct((m, d), BF16)],
        compiler_params=_params(("parallel",)),
        name="out_proj",
    )(x, att, s5o, w_out_b, norm_ffn.reshape(1, d))


def _ffn_init(h_ref, o_ref):
    @pl.when(pl.program_id(1) == 0)
    def _():
        o_ref[...] = h_ref[...]


def _ffn_down(acts, wd_ref, o_ref):
    tc = acts[0].shape[1]
    part = jnp.dot(acts[0], wd_ref[0:tc, :], preferred_element_type=F32)
    for n in range(1, len(acts)):
        part = part + jnp.dot(acts[n], wd_ref[n * tc:(n + 1) * tc, :],
                              preferred_element_type=F32)
    o_ref[...] += part


def _ffn_prompt_kernel(f_ref, h_ref, wg_ref, wu_ref, wd_ref, cw_ref, cb_ref,
                       o_ref, buf_ref, hgx_ref, prev_ref, *, tm, tc):
    i = pl.program_id(0)
    j = pl.program_id(1)
    halo = V7X_SUBLANES
    tf = wg_ref.shape[1]
    _ffn_init(h_ref, o_ref)

    @pl.when(i == 0)
    def _():
        hgx_ref[0:halo, :] = jnp.zeros((halo, tf), F32)

    @pl.when(i > 0)
    def _():
        hgx_ref[0:halo, :] = prev_ref[j]

    f = f_ref[...]
    blocks = [slice(c0, c0 + tc) for c0 in range(0, tf, tc)]
    hgs = [None] * len(blocks)
    hus = [None] * len(blocks)

    def up(n):
        hgs[n] = jnp.dot(f, wg_ref[:, blocks[n]], preferred_element_type=F32)
        hus[n] = jnp.dot(f, wu_ref[:, blocks[n]], preferred_element_type=F32)

    up(0)
    part = None
    for n, cs in enumerate(blocks):
        if n + 1 < len(blocks):
            up(n + 1)
        hg = hgs[n]
        hgx_ref[halo:, cs] = hg
        cw = cw_ref[:, cs]
        conv = cb_ref[:, cs] + cw[2:3] * hg
        for tap in range(CONV_W - 1):
            off = halo - (CONV_W - 1) + tap
            conv = conv + cw[tap:tap + 1] * hgx_ref[pl.ds(off, tm), cs]
        act = (jax.nn.gelu(conv) * hus[n]).astype(BF16)
        down = jnp.dot(act, wd_ref[cs, :], preferred_element_type=F32)
        part = down if part is None else part + down
    tail = hgx_ref[tm:tm + halo, :]
    prev_ref[j] = tail
    buf_ref[...] = tail
    o_ref[...] += part


def _ffn_sample_kernel(f_ref, h_ref, wg_ref, wu_ref, wd_ref, cw_ref, cb_ref, b0_ref, b1_ref,
                       o_ref, hg_ref):
    _ffn_init(h_ref, o_ref)
    f = f_ref[...]
    hg = jnp.dot(f, wg_ref[...], preferred_element_type=F32)
    hu = jnp.dot(f, wu_ref[...], preferred_element_type=F32)
    hg_ref[...] = hg
    cw = cw_ref[...]
    conv = cb_ref[...] + cw[0:1] * b0_ref[...] + cw[1:2] * b1_ref[...] + cw[2:3] * hg
    _ffn_down([(jax.nn.gelu(conv) * hu).astype(BF16)], wd_ref, o_ref)


def _ffn(f, h, w_gate_b, w_up_b, w_down_b, conv_w, conv_b, conv_buf, tm, tf):
    m, d = f.shape
    dff = w_gate_b.shape[1]
    tm = _tile(m, tm)
    tf = _tile(dff, tf)
    n_ff = dff // tf
    row = lambda i, j: (i, 0)
    col = lambda i, j: (0, j)
    common_in = [pl.BlockSpec((tm, d), row), pl.BlockSpec((tm, d), row),
                 pl.BlockSpec((d, tf), col), pl.BlockSpec((d, tf), col),
                 pl.BlockSpec((tf, d), lambda i, j: (j, 0)),
                 pl.BlockSpec((CONV_W, tf), col), pl.BlockSpec((1, tf), col)]
    common_args = (f, h, w_gate_b, w_up_b, w_down_b, conv_w.astype(F32),
                   conv_b.reshape(1, dff).astype(F32))
    if conv_buf is None:
        halo = V7X_SUBLANES
        kern = functools.partial(_ffn_prompt_kernel, tm=tm, tc=_tile(tf, V7X_MXU_COLS))
        out, tail = pl.pallas_call(
            kern,
            grid=(m // tm, n_ff),
            in_specs=common_in,
            out_specs=[pl.BlockSpec((tm, d), row),
                       pl.BlockSpec((None, halo, tf), lambda i, j: (i, 0, j))],
            out_shape=[jax.ShapeDtypeStruct((m, d), F32),
                       jax.ShapeDtypeStruct((m // tm, halo, dff), F32)],
            scratch_shapes=[pltpu.VMEM((tm + halo, tf), F32),
                            pltpu.VMEM((n_ff, halo, tf), F32)],
            compiler_params=_params(("arbitrary", "arbitrary")),
            name="ffn_prompt",
        )(*common_args)
        return out, tail[-1, halo - (CONV_W - 1):]
    b0 = conv_buf[:, 0, :].astype(F32)
    b1 = conv_buf[:, 1, :].astype(F32)
    out, hg = pl.pallas_call(
        _ffn_sample_kernel,
        grid=(m // tm, n_ff),
        in_specs=common_in + [pl.BlockSpec((tm, tf), lambda i, j: (i, j))] * 2,
        out_specs=[pl.BlockSpec((tm, d), row), pl.BlockSpec((tm, tf), lambda i, j: (i, j))],
        out_shape=[jax.ShapeDtypeStruct((m, d), F32), jax.ShapeDtypeStruct((m, dff), F32)],
        compiler_params=_params(("parallel", "arbitrary")),
        name="ffn_sample",
    )(*common_args, b0, b1)
    return out, jnp.stack([b1, hg], axis=1)


def _ple_kernel(h_ref, p_ref, wg_ref, wp_ref, g_ref, o_ref):
    h = h_ref[...]
    gate = jax.nn.sigmoid(jnp.dot(h.astype(BF16), wg_ref[...], preferred_element_type=F32))
    e = jnp.dot(p_ref[...].astype(BF16), wp_ref[...], preferred_element_type=F32)
    ms = jnp.mean(e * e, axis=-1, keepdims=True)
    o_ref[...] = h + gate * (e * lax.rsqrt(ms + EPS) * g_ref[...])


def _ple(h, p, w_gate_b, w_proj_b, ple_norm, tm):
    m, d = h.shape
    pd = p.shape[1]
    tm = _tile(m, tm)
    row = lambda i: (i, 0)
    const = lambda i: (0, 0)
    return pl.pallas_call(
        _ple_kernel,
        grid=(m // tm,),
        in_specs=[pl.BlockSpec((tm, d), row), pl.BlockSpec((tm, pd), row),
                  pl.BlockSpec((d, d), const), pl.BlockSpec((pd, d), const),
                  pl.BlockSpec((1, d), const)],
        out_specs=pl.BlockSpec((tm, d), row),
        out_shape=jax.ShapeDtypeStruct((m, d), F32),
        compiler_params=_params(("parallel",)),
        name="ple_gate",
    )(h, p, w_gate_b, w_proj_b, ple_norm.reshape(1, d))


def kernel(x_prompt, x_sample, cache_k, cache_v, state_s5_re, state_s5_im, state_conv, page_table, p_prompt, p_sample, norm_mix, w_in, q_norm, k_norm, lam_q1, lam_k1, lam_q2, lam_k2, subln, s5_a_re, s5_a_im, s5_b_re, s5_b_im, s5_c_re, s5_c_im, s5_d, s5_log_dt, w_glu, b_glu, w_out, norm_ffn, w_gate, w_up, conv_w, conv_b, w_down, w_ple_gate, w_ple_proj, ple_norm):
    depth = w_in.shape[0]
    b, s, d = x_prompt.shape
    db, t_new, _ = x_sample.shape
    assert b == 1 and t_new == 1
    page = cache_k.shape[2]
    past_len = page_table.shape[1] * page
    hp = x_prompt.reshape(s, d)
    hs = x_sample.reshape(db, d)
    pos_p = np.arange(s)
    pos_s = np.full((db,), past_len)
    n_pool = cache_k.shape[1]
    ck = jnp.transpose(cache_k, (0, 1, 3, 4, 5, 2)).reshape(depth * n_pool, -1, page)
    cv = cache_v.reshape(depth * n_pool, page * N_HEADS, V_DIM)
    outs = [[] for _ in range(10)]
    for i in range(depth):
        lam_init = 0.8 - 0.6 * math.exp(-0.3 * i)
        lam_vec = jnp.stack([lam_q1[i], lam_k1[i], lam_q2[i], lam_k2[i]]).astype(F32)
        w_in_b = w_in[i].astype(BF16)
        w_glu_b = w_glu[i].astype(BF16)
        w_out_b = w_out[i].astype(BF16)
        w_gate_b = w_gate[i].astype(BF16)
        w_up_b = w_up[i].astype(BF16)
        w_down_b = w_down[i].astype(BF16)
        w_pg_b = w_ple_gate[i].astype(BF16)
        w_pp_b = w_ple_proj[i].astype(BF16)
        tabs = _s5_tables(s5_a_re[i], s5_a_im[i], s5_b_re[i], s5_b_im[i], s5_c_re[i],
                          s5_c_im[i], s5_d[i], s5_log_dt[i], S5_TILE // V7X_SUBLANES)

        q, k32, kb, v32, vt, u = _project(hp, pos_p, norm_mix[i], w_in_b, q_norm[i], k_norm[i],
                                          512, True)
        qs, k32s, kbs, v32s, vbs, us = _project(hs, pos_s, norm_mix[i], w_in_b, q_norm[i],
                                                k_norm[i], 512, False)
        att, att_s = _attention(q, kb, vt, qs, kbs, vbs, ck, cv, page_table + i * n_pool,
                                lam_vec, subln[i], lam_init, 1024)

        s5o, s_re, s_im = _s5_prompt(u, tabs, w_glu_b, b_glu[i], S5_TILE)
        h1, f = _outproj(hp, att, s5o, w_out_b, norm_ffn[i], 512)
        h2, cbuf = _ffn(f, h1, w_gate_b, w_up_b, w_down_b, conv_w[i], conv_b[i], None, 512, 512)
        hp = _ple(h2, p_prompt[i].reshape(s, -1), w_pg_b, w_pp_b, ple_norm[i], 512)
        outs[0].append(k32.reshape(b, s, N_HEADS, 2, HEAD_DIM))
        outs[1].append(v32.reshape(b, s, N_HEADS, V_DIM))
        outs[2].append(s_re[None])
        outs[3].append(s_im[None])
        outs[4].append(cbuf[None])

        s5o, s_re, s_im = _s5_sample(us, state_s5_re[i], state_s5_im[i], tabs, w_glu_b, b_glu[i])
        h1, f = _outproj(hs, att_s, s5o, w_out_b, norm_ffn[i], 512)
        h2, cbuf = _ffn(f, h1, w_gate_b, w_up_b, w_down_b, conv_w[i], conv_b[i], state_conv[i],
                        512, 512)
        hs = _ple(h2, p_sample[i].reshape(db, -1), w_pg_b, w_pp_b, ple_norm[i], 512)
        outs[5].append(k32s.reshape(db, t_new, N_HEADS, 2, HEAD_DIM))
        outs[6].append(v32s.reshape(db, t_new, N_HEADS, V_DIM))
        outs[7].append(s_re)
        outs[8].append(s_im)
        outs[9].append(cbuf)
    return (hp.reshape(b, s, d), hs.reshape(db, t_new, d)) + tuple(jnp.stack(o) for o in outs)
```

```python
import functools
import math

import jax
import jax.numpy as jnp
import numpy as np
from jax import lax
from jax.experimental import pallas as pl
from jax.experimental.pallas import tpu as pltpu

N_HEADS = 8
HEAD_DIM = 64
V_DIM = 2 * HEAD_DIM
ROT_DIM = HEAD_DIM // 4
ROPE_THETA = 500000.0
S5_CH = 16
S5_STATE = 64
CONV_W = 3
EPS = 1e-6
Q_SCALE = math.log2(math.e) * HEAD_DIM ** -0.5

V7X_LANES = 128
V7X_SUBLANES = 8
V7X_MXU_COLS = 256
VMEM_LIMIT_BYTES = 56 * 1024 * 1024
ATTN_VMEM_LIMIT_BYTES = 60 * 1024 * 1024

S5_SLAB_GROUPS = V7X_LANES // S5_CH
S5_SLAB_STATE = S5_SLAB_GROUPS * S5_STATE

ROW_TILE = 512
ATTN_TILE = 1024
FFN_COL_TILE = 512
S5_TILE = 256

F32 = jnp.float32
BF16 = jnp.bfloat16


def _tile(n, pref):
    t = min(n, pref)
    assert n % t == 0, (n, t)
    return t


def _params(sem, vmem=VMEM_LIMIT_BYTES):
    return pltpu.CompilerParams(dimension_semantics=sem, vmem_limit_bytes=vmem)


def _qk_post(p, gain, cos, sin_lo, sin_hi, seg):
    outs = []
    for c in range(p.shape[1] // V7X_LANES):
        pc = p[:, c * V7X_LANES:(c + 1) * V7X_LANES]
        ms = jnp.dot((pc * pc).astype(BF16), seg, preferred_element_type=F32)
        y = pc * lax.rsqrt(ms + EPS) * gain
        y = (y * cos
             + pltpu.roll(y, V7X_LANES - ROT_DIM // 2, 1) * sin_lo
             + pltpu.roll(y, ROT_DIM // 2, 1) * sin_hi)
        outs.append(y)
    return jnp.concatenate(outs, axis=1)


def _proj_kernel(x_ref, g_ref, w_ref, qg_ref, kg_ref, cos_ref, slo_ref, shi_ref, seg_ref,
                 q_ref, k32_ref, kb_ref, v32_ref, vb_ref, u_ref, *, v_transposed, bw):
    width = q_ref.shape[1]
    nb = width // bw
    x = x_ref[...]
    ms = jnp.mean(x * x, axis=-1, keepdims=True)
    xn = (x * lax.rsqrt(ms + EPS) * g_ref[...]).astype(BF16)
    rope = (cos_ref[...], slo_ref[...], shi_ref[...], seg_ref[...])

    def block(c):
        return jnp.dot(xn, w_ref[:, c * bw:(c + 1) * bw], preferred_element_type=F32)

    for b in range(nb):
        cols = slice(b * bw, (b + 1) * bw)
        q = _qk_post(block(b), qg_ref[...], *rope)
        q_ref[:, cols] = (q * Q_SCALE).astype(BF16)
    for b in range(nb):
        cols = slice(b * bw, (b + 1) * bw)
        k = _qk_post(block(nb + b), kg_ref[...], *rope)
        k32_ref[:, cols] = k
        kb_ref[:, cols] = k.astype(BF16)
    for b in range(nb):
        cols = slice(b * bw, (b + 1) * bw)
        v = block(2 * nb + b)
        v32_ref[:, cols] = v
        if v_transposed:
            for hh in range(bw // V_DIM):
                vb_ref[b * (bw // V_DIM) + hh, 0] = (
                    v[:, hh * V_DIM:(hh + 1) * V_DIM].T.astype(BF16))
        else:
            vb_ref[:, cols] = v.astype(BF16)
    for b in range(nb):
        u_ref[:, b * bw:(b + 1) * bw] = block(3 * nb + b)


def _rope_tables(pos):
    half = ROT_DIM // 2
    inv = ROPE_THETA ** (-(np.arange(half, dtype=np.float64) * 2.0 / ROT_DIM))
    ang = np.asarray(pos, np.float64)[:, None] * inv[None, :]
    cos, sin = np.cos(ang), np.sin(ang)
    t = ang.shape[0]
    ones = np.ones((t, HEAD_DIM - ROT_DIM))
    zeros = np.zeros((t, HEAD_DIM - ROT_DIM))
    zh = np.zeros((t, half))
    cos_t = np.concatenate([cos, cos, ones], axis=1)
    lo_t = np.concatenate([-sin, zh, zeros], axis=1)
    hi_t = np.concatenate([zh, sin, zeros], axis=1)
    rep = V7X_LANES // HEAD_DIM
    return tuple(jnp.asarray(np.tile(tab, (1, rep)), F32) for tab in (cos_t, lo_t, hi_t))


def _project(x, pos, norm_mix, w_in_b, q_norm, k_norm, tm, v_transposed):
    m, d = x.shape
    width = w_in_b.shape[1] // 4
    tm = _tile(m, tm)
    if v_transposed:
        vb_spec = pl.BlockSpec((N_HEADS, 1, V_DIM, tm), lambda i: (0, i, 0, 0))
        vb_shape = jax.ShapeDtypeStruct((N_HEADS, m // tm, V_DIM, tm), BF16)
    else:
        vb_spec = pl.BlockSpec((tm, width), lambda i: (i, 0))
        vb_shape = jax.ShapeDtypeStruct((m, width), BF16)
    cos_t, lo_t, hi_t = _rope_tables(pos)
    rep = V7X_LANES // HEAD_DIM
    qg = jnp.tile(q_norm.reshape(1, HEAD_DIM), (1, rep))
    kg = jnp.tile(k_norm.reshape(1, HEAD_DIM), (1, rep))
    lane = np.arange(V7X_LANES)
    seg = jnp.asarray((lane[:, None] // HEAD_DIM == lane[None, :] // HEAD_DIM) / HEAD_DIM, BF16)
    row = lambda i: (i, 0)
    const = lambda i: (0, 0)
    out_spec = pl.BlockSpec((tm, width), row)
    kern = functools.partial(_proj_kernel, v_transposed=v_transposed,
                             bw=_tile(width, 2 * V7X_MXU_COLS))
    return pl.pallas_call(
        kern,
        grid=(m // tm,),
        in_specs=[
            pl.BlockSpec((tm, d), row),
            pl.BlockSpec((1, d), const),
            pl.BlockSpec((d, 4 * width), const, pipeline_mode=pl.Buffered(1)),
            pl.BlockSpec((1, V7X_LANES), const),
            pl.BlockSpec((1, V7X_LANES), const),
            pl.BlockSpec((tm, V7X_LANES), row),
            pl.BlockSpec((tm, V7X_LANES), row),
            pl.BlockSpec((tm, V7X_LANES), row),
            pl.BlockSpec((V7X_LANES, V7X_LANES), const),
        ],
        out_specs=[out_spec] * 4 + [vb_spec, out_spec],
        out_shape=[
            jax.ShapeDtypeStruct((m, width), BF16),
            jax.ShapeDtypeStruct((m, width), F32),
            jax.ShapeDtypeStruct((m, width), BF16),
            jax.ShapeDtypeStruct((m, width), F32),
            vb_shape,
            jax.ShapeDtypeStruct((m, width), F32),
        ],
        compiler_params=_params(("parallel",)),
        name="in_proj",
    )(x, norm_mix.reshape(1, d), w_in_b, qg, kg, cos_t, lo_t, hi_t, seg)


def _lambda(lam_ref, lam_init):
    lv = lam_ref[...]
    l1 = jnp.sum(lv[0:1] * lv[1:2], axis=-1, keepdims=True)
    l2 = jnp.sum(lv[2:3] * lv[3:4], axis=-1, keepdims=True)
    return jnp.exp(l1) - jnp.exp(l2) + lam_init


def _subln(out, sub, lam_init):
    ms = jnp.mean(out * out, axis=-1, keepdims=True)
    return out * lax.rsqrt(ms + EPS) * sub * (1.0 - lam_init)


def _prompt_chunk(q_ref, k_ref, vt_ref, acc_ref, st_ref, ml_ref, qi, kc, *, tq, masked):
    tv = vt_ref.shape[-1]
    n_sub = tq // tv
    q = q_ref[...].astype(F32)
    lane = lax.broadcasted_iota(jnp.int32, q.shape, 1)
    qs = jnp.concatenate([jnp.where(lane < HEAD_DIM, q, 0.0),
                          jnp.where(lane >= HEAD_DIM, q, 0.0)], axis=0).astype(BF16)

    def scores(j):
        k0 = pl.multiple_of(kc * tq + j * tv, tv)
        st_ref[j] = lax.dot_general(k_ref[pl.ds(k0, tv), :], qs, (((1,), (1,)), ((), ())),
                                    preferred_element_type=F32)

    m, l = ml_ref[0:1, :], ml_ref[1:2, :]
    scores(0)
    for j in range(n_sub):
        if j + 1 < n_sub:
            scores(j + 1)
        st = st_ref[j]
        if masked:
            key = lax.broadcasted_iota(jnp.int32, st.shape, 0) + j * tv
            qry = lax.broadcasted_iota(jnp.int32, st.shape, 1)
            qry = jnp.where(qry >= tq, qry - tq, qry)
            st = jnp.where(key <= qry, st, -jnp.inf)
        m_new = jnp.maximum(m, jnp.max(st, axis=0, keepdims=True))
        alpha = jnp.exp2(m - m_new)
        p = jnp.exp2(st - m_new)
        l = alpha * l + jnp.sum(p, axis=0, keepdims=True)
        m = m_new
        pv = jnp.dot(vt_ref[kc * n_sub + j], p.astype(BF16), preferred_element_type=F32)
        acc_ref[...] = alpha * acc_ref[...] + pv
    ml_ref[0:1, :] = m
    ml_ref[1:2, :] = l


def _prompt_finish(acc_ref, ml_ref, lam_ref, sub_ref, o_ref, *, tq, lam_init):
    lam = _lambda(lam_ref, lam_init)
    o = acc_ref[...] / ml_ref[1:2, :]
    out = o[:, :tq] - lam * o[:, tq:]
    ms = jnp.mean(out * out, axis=0, keepdims=True)
    y = out * lax.rsqrt(ms + EPS) * sub_ref[...] * (1.0 - lam_init)
    o_ref[...] = y.T.astype(o_ref.dtype)


def _sample_init(q_ref, qm_ref, m_ref, l_ref, acc_ref):
    n_rows, width = qm_ref.shape
    q = jnp.broadcast_to(q_ref[0].astype(F32), (n_rows, width))
    r = lax.broadcasted_iota(jnp.int32, (n_rows, width), 0)
    c = lax.broadcasted_iota(jnp.int32, (n_rows, width), 1)
    keep = (c >= r * HEAD_DIM) & (c < (r + 1) * HEAD_DIM)
    qm_ref[...] = jnp.where(keep, q, 0.0).astype(BF16)
    m_ref[...] = jnp.full(m_ref.shape, -jnp.inf, F32)
    l_ref[...] = jnp.zeros(l_ref.shape, F32)
    acc_ref[...] = jnp.zeros(acc_ref.shape, F32)


def _sample_pages(k_refs, v_refs, exp_ref, qm_ref, m_ref, l_ref, acc_ref):
    n_pg = len(k_refs)
    n_rows = 2 * N_HEADS
    page = k_refs[0].shape[1]
    qm = qm_ref[...]
    s = jnp.concatenate(
        [jnp.dot(qm, k_refs[g][...].astype(BF16), preferred_element_type=F32)
         for g in range(n_pg)], axis=1)
    m_prev = m_ref[...]
    m_new = jnp.maximum(m_prev, jnp.max(s, axis=1, keepdims=True))
    alpha = jnp.exp2(m_prev - m_new)
    p = jnp.exp2(s - m_new)
    l_new = alpha * l_ref[...] + jnp.sum(p, axis=1, keepdims=True)
    pb = p.astype(BF16)
    p_rows = jnp.concatenate([pb[:, g * page:(g + 1) * page] for g in range(n_pg)], axis=0)
    p_wide = jnp.dot(p_rows, exp_ref[...], preferred_element_type=F32)
    r = lax.broadcasted_iota(jnp.int32, p_wide.shape, 0)
    c = lax.broadcasted_iota(jnp.int32, p_wide.shape, 1)
    own = (c % N_HEADS) == ((r % n_rows) // 2)
    p_wide = jnp.where(own, p_wide, 0.0).astype(BF16)
    acc = alpha * acc_ref[...]
    for g in range(n_pg):
        acc = acc + jnp.dot(p_wide[g * n_rows:(g + 1) * n_rows], v_refs[g][...].astype(BF16),
                            preferred_element_type=F32)
    m_ref[...] = m_new
    l_ref[...] = l_new
    acc_ref[...] = acc


def _sample_finish(kn_ref, vn_ref, lam_ref, sub_ref, o_ref, qm_ref, m_ref, l_ref, acc_ref,
                   *, lam_init):
    kn = kn_ref[0].astype(F32)
    vn = vn_ref[0].astype(F32)
    s_new = jnp.sum(qm_ref[...].astype(F32) * kn, axis=1, keepdims=True)
    m_old = m_ref[...]
    m_fin = jnp.maximum(m_old, s_new)
    a = jnp.exp2(m_old - m_fin)
    p_new = jnp.exp2(s_new - m_fin)
    l_fin = a * l_ref[...] + p_new
    acc_fin = a * acc_ref[...] + p_new.astype(BF16).astype(F32) * vn
    o = acc_fin / l_fin
    o1 = jnp.concatenate([o[2 * h:2 * h + 1] for h in range(N_HEADS)], axis=0)
    o2 = jnp.concatenate([o[2 * h + 1:2 * h + 2] for h in range(N_HEADS)], axis=0)
    lam = _lambda(lam_ref, lam_init)
    o_ref[0] = _subln(o1 - lam * o2, sub_ref[...], lam_init)


def _attn_kernel(sh_ref, sq_ref, sk_ref, pt_ref, *refs, n_pg, tq, lam_init, n_sample_steps,
                 steps_per_seq):
    del sh_ref, pt_ref
    k_pages = refs[:n_pg]
    v_pages = refs[n_pg:2 * n_pg]
    (q_ref, k_ref, vt_ref, lam_ref, subc_ref, qs_ref, kn_ref, vn_ref, exp_ref, subr_ref,
     o_ref, os_ref, acc_ref, st_ref, ml_ref, qm_ref, ms_ref, ls_ref, accs_ref) = refs[2 * n_pg:]
    t = pl.program_id(0)
    qi = sq_ref[t]
    kc = sk_ref[t]
    valid = t < n_sample_steps
    s_id = t % steps_per_seq

    @pl.when(kc == 0)
    def _():
        acc_ref[...] = jnp.zeros(acc_ref.shape, F32)
        ml_ref[0:1, :] = jnp.full((1, ml_ref.shape[1]), -jnp.inf, F32)
        ml_ref[1:2, :] = jnp.zeros((1, ml_ref.shape[1]), F32)

    @pl.when(valid & (s_id == 0))
    def _():
        _sample_init(qs_ref, qm_ref, ms_ref, ls_ref, accs_ref)

    chunk = functools.partial(_prompt_chunk, q_ref, k_ref, vt_ref, acc_ref, st_ref, ml_ref,
                              qi, kc, tq=tq)

    @pl.when(kc < qi)
    def _():
        chunk(masked=False)

    @pl.when(kc == qi)
    def _():
        chunk(masked=True)
        _prompt_finish(acc_ref, ml_ref, lam_ref, subc_ref, o_ref, tq=tq, lam_init=lam_init)

    @pl.when(valid)
    def _():
        _sample_pages(k_pages, v_pages, exp_ref, qm_ref, ms_ref, ls_ref, accs_ref)

    @pl.when(valid & (s_id == steps_per_seq - 1))
    def _():
        _sample_finish(kn_ref, vn_ref, lam_ref, subr_ref, os_ref, qm_ref, ms_ref, ls_ref,
                       accs_ref, lam_init=lam_init)


def _attention(q, k, vt, qs, k_new, v_new, ck, cv, page_idx, lam_vec, subln, lam_init, tq):
    s, width = q.shape
    db = qs.shape[0]
    tv = vt.shape[-1]
    tq = _tile(s, max(tq, tv))
    assert tq % tv == 0
    nq = s // tq
    page = ck.shape[2]
    n_pages = page_idx.shape[1]
    n_rows = 2 * N_HEADS

    sched = [(h, qi, kc) for h in range(N_HEADS) for qi in range(nq) for kc in range(qi + 1)]
    n_steps = len(sched)
    sh, sq, sk = (jnp.asarray([e[n] for e in sched], jnp.int32) for n in range(3))
    n_pg = next(g for g in range(1, n_pages + 1)
                if n_pages % g == 0 and db * (n_pages // g) <= n_steps)
    steps_per_seq = n_pages // n_pg
    n_sample_steps = db * steps_per_seq

    tok = np.arange(page)
    expand = jnp.asarray(tok[:, None] == (np.arange(page * N_HEADS)[None, :] // N_HEADS), BF16)
    vn = jnp.repeat(v_new.reshape(db, N_HEADS, V_DIM), 2, axis=1)

    def seq_of(t):
        ts = jnp.minimum(t, n_sample_steps - 1)
        return ts // steps_per_seq, ts % steps_per_seq

    def page_spec(shape, g):
        def index(t, sh, sq, sk, pt):
            b, s_id = seq_of(t)
            return (pt[b, s_id * n_pg + g], 0, 0)
        return pl.BlockSpec((None,) + shape, index)

    const = lambda t, sh, sq, sk, pt: (0, 0)
    tile = lambda t, sh, sq, sk, pt: (sq[t], sh[t])
    seq = lambda t, sh, sq, sk, pt: (seq_of(t)[0], 0, 0)
    kern = functools.partial(_attn_kernel, n_pg=n_pg, tq=tq, lam_init=lam_init,
                             n_sample_steps=n_sample_steps, steps_per_seq=steps_per_seq)
    att, att_s = pl.pallas_call(
        kern,
        grid_spec=pltpu.PrefetchScalarGridSpec(
            num_scalar_prefetch=4,
            grid=(n_steps,),
            in_specs=([page_spec((width, page), g) for g in range(n_pg)]
                      + [page_spec((page * N_HEADS, V_DIM), g) for g in range(n_pg)]
                      + [pl.BlockSpec((tq, V_DIM), tile),
                         pl.BlockSpec((s, V_DIM), lambda t, sh, sq, sk, pt: (0, sh[t])),
                         pl.BlockSpec((None, s // tv, V_DIM, tv),
                                      lambda t, sh, sq, sk, pt: (sh[t], 0, 0, 0)),
                         pl.BlockSpec((4, HEAD_DIM), const),
                         pl.BlockSpec((V_DIM, 1), const),
                         pl.BlockSpec((1, 1, width), seq), pl.BlockSpec((1, 1, width), seq),
                         pl.BlockSpec((1, n_rows, V_DIM), seq),
                         pl.BlockSpec(expand.shape, const),
                         pl.BlockSpec((1, V_DIM), const)]),
            out_specs=[pl.BlockSpec((tq, V_DIM), tile),
                       pl.BlockSpec((1, N_HEADS, V_DIM), seq)],
            scratch_shapes=[pltpu.VMEM((V_DIM, 2 * tq), F32),
                            pltpu.VMEM((tq // tv, tv, 2 * tq), F32),
                            pltpu.VMEM((2, 2 * tq), F32),
                            pltpu.VMEM((n_rows, width), BF16),
                            pltpu.VMEM((n_rows, 1), F32), pltpu.VMEM((n_rows, 1), F32),
                            pltpu.VMEM((n_rows, V_DIM), F32)],
        ),
        out_shape=[jax.ShapeDtypeStruct((s, width), BF16),
                   jax.ShapeDtypeStruct((db, N_HEADS, V_DIM), F32)],
        compiler_params=_params(("arbitrary",), ATTN_VMEM_LIMIT_BYTES),
        name="attention",
    )(sh, sq, sk, page_idx, *([ck] * n_pg), *([cv] * n_pg),
      q, k, vt, lam_vec, subln.reshape(V_DIM, 1),
      qs.reshape(db, 1, width), k_new.reshape(db, 1, width), vn, expand,
      subln.reshape(1, V_DIM))
    return att, att_s.reshape(db, width).astype(BF16)


def _s5_tables(a_re, a_im, b_re, b_im, c_re, c_im, d, log_dt, seg_len):
    g, p = a_re.shape
    n_slab = g // S5_SLAB_GROUPS
    dt = jnp.exp(log_dt.astype(F32))[:, None]
    mag = jnp.exp(dt * a_re)
    ab_re = mag * jnp.cos(dt * a_im)
    ab_im = mag * jnp.sin(dt * a_im)
    den = a_re * a_re + a_im * a_im
    n_re = ab_re - 1.0
    f_re = (n_re * a_re + ab_im * a_im) / den
    f_im = (ab_im * a_re - n_re * a_im) / den
    bb_re = f_re[..., None] * b_re - f_im[..., None] * b_im
    bb_im = f_re[..., None] * b_im + f_im[..., None] * b_re
    eye = jnp.eye(S5_SLAB_GROUPS, dtype=F32)

    def in_mat(bb):
        bb = bb.reshape(n_slab, S5_SLAB_GROUPS, p, S5_CH)
        return jnp.einsum('jgpc,gh->jgchp', bb, eye).reshape(n_slab, V7X_LANES, S5_SLAB_STATE)

    def out_mat(cc):
        cc = cc.reshape(n_slab, S5_SLAB_GROUPS, S5_CH, p)
        return jnp.einsum('jgcp,gh->jgphc', cc, eye).reshape(n_slab, S5_SLAB_STATE, V7X_LANES)

    w_in = jnp.concatenate([in_mat(bb_re), in_mat(bb_im)], axis=2).astype(BF16)
    w_out = jnp.concatenate([out_mat(c_re.astype(F32)), out_mat(-c_im.astype(F32))],
                            axis=1).astype(BF16)

    def apow(ns):
        ns = jnp.asarray(ns, F32)[:, None, None]
        mag = jnp.exp(ns * (dt * a_re))
        ang = ns * (dt * a_im)
        flat = lambda t: jnp.swapaxes(t.reshape(-1, n_slab, S5_SLAB_STATE), 0, 1)
        return flat(mag * jnp.cos(ang)), flat(mag * jnp.sin(ang))

    def block_scan_tables(step):
        p_re, p_im = apow(step * np.arange(1, V7X_SUBLANES + 1))
        rows = np.arange(V7X_SUBLANES)[None, :, None]
        tabs = [p_re, p_im]
        for k in (1, 2, 4):
            tabs.append(jnp.where(rows >= k, p_re[:, k - 1:k], 0.0))
            tabs.append(jnp.where(rows >= k, p_im[:, k - 1:k], 0.0))
        return jnp.stack(tabs, axis=1)

    scan = block_scan_tables(1)
    qpow = jnp.stack(apow(np.arange(1, seg_len + 1)), axis=1)
    seg = block_scan_tables(seg_len)
    dvec = d.astype(F32).reshape(n_slab, 1, V7X_LANES)
    return w_in, w_out, scan, dvec, qpow, seg


def _glu(y, wglu_ref, bglu_ref):
    g = jax.nn.gelu(y)
    z = jnp.dot(g.astype(BF16), wglu_ref[...], preferred_element_type=F32) + bglu_ref[...]
    return g * jax.nn.sigmoid(z)


def _s5_prompt_kernel(u_ref, perm_ref, permt_ref, win_ref, wout_ref, qpow_ref, seg_ref, d_ref,
                      wglu_ref, bglu_ref, o_ref, st_ref, x_scr, y_scr, c_scr, *, tt):
    t = pl.program_id(0)
    n_slab = win_ref.shape[0]
    ns = S5_SLAB_STATE
    sub = V7X_SUBLANES
    seg_len = tt // sub

    @pl.when(t == 0)
    def _():
        c_scr[...] = jnp.zeros(c_scr.shape, F32)

    u = u_ref[...]
    ub = jnp.dot(perm_ref[...], u.astype(BF16), preferred_element_type=F32).astype(BF16)
    row = lax.broadcasted_iota(jnp.int32, (sub, ns), 0)

    for j in range(n_slab):
        lanes = slice(j * V7X_LANES, (j + 1) * V7X_LANES)
        x_scr[...] = jnp.dot(ub[:, lanes], win_ref[j], preferred_element_type=F32)
        a_re, a_im = qpow_ref[j, 0, 0:1, :], qpow_ref[j, 1, 0:1, :]

        xr, xi = x_scr[0:sub, 0:ns], x_scr[0:sub, ns:2 * ns]
        for i in range(1, seg_len):
            rows = slice(i * sub, (i + 1) * sub)
            xr, xi = (a_re * xr - a_im * xi + x_scr[rows, 0:ns],
                      a_re * xi + a_im * xr + x_scr[rows, ns:2 * ns])
            x_scr[rows, 0:ns] = xr
            x_scr[rows, ns:2 * ns] = xi

        cr, ci = c_scr[j:j + 1, 0:ns], c_scr[j:j + 1, ns:2 * ns]
        er, ei = xr, xi
        for n, k in enumerate((1, 2, 4)):
            mr, mi = seg_ref[j, 2 + 2 * n], seg_ref[j, 3 + 2 * n]
            sr, si = pltpu.roll(er, k, 0), pltpu.roll(ei, k, 0)
            er, ei = er + mr * sr - mi * si, ei + mr * si + mi * sr
        p_re, p_im = seg_ref[j, 0], seg_ref[j, 1]
        er, ei = er + p_re * cr - p_im * ci, ei + p_re * ci + p_im * cr
        c_scr[j:j + 1, 0:ns] = er[sub - 1:sub]
        c_scr[j:j + 1, ns:2 * ns] = ei[sub - 1:sub]
        sr = jnp.where(row == 0, cr, pltpu.roll(er, 1, 0))
        si = jnp.where(row == 0, ci, pltpu.roll(ei, 1, 0))

        for i in range(seg_len):
            rows = slice(i * sub, (i + 1) * sub)
            q_re, q_im = qpow_ref[j, 0, i:i + 1, :], qpow_ref[j, 1, i:i + 1, :]
            x_scr[rows, 0:ns] += q_re * sr - q_im * si
            x_scr[rows, ns:2 * ns] += q_re * si + q_im * sr
        y_scr[:, lanes] = jnp.dot(x_scr[...].astype(BF16), wout_ref[j],
                                  preferred_element_type=F32)

    yp = y_scr[...]
    hi = yp.astype(BF16)
    rest = yp - hi.astype(F32)
    mid = rest.astype(BF16)
    low = (rest - mid.astype(F32)).astype(BF16)
    permt = permt_ref[...]
    y = (jnp.dot(permt, hi, preferred_element_type=F32)
         + jnp.dot(permt, mid, preferred_element_type=F32)
         + jnp.dot(permt, low, preferred_element_type=F32))
    y = y + d_ref[...] * u
    o_ref[...] = _glu(y, wglu_ref, bglu_ref).astype(o_ref.dtype)
    st_ref[...] = c_scr[...]


def _s5_prompt(u, tabs, w_glu_b, b_glu, tt):
    s, width = u.shape
    w_in, w_out, _, dvec, qpow, seg = tabs
    n_slab = w_in.shape[0]
    tt = _tile(s, tt)
    assert qpow.shape[2] * V7X_SUBLANES == tt
    seg_len = tt // V7X_SUBLANES
    src = (np.arange(tt) % V7X_SUBLANES) * seg_len + np.arange(tt) // V7X_SUBLANES
    perm_np = src[:, None] == np.arange(tt)[None, :]
    perm, perm_t = jnp.asarray(perm_np, BF16), jnp.asarray(perm_np.T, BF16)
    full = lambda a: pl.BlockSpec(a.shape, lambda t: (0,) * a.ndim)
    bglu = b_glu.reshape(1, width).astype(F32)
    drow = dvec.reshape(1, width)
    kern = functools.partial(_s5_prompt_kernel, tt=tt)
    consts = (perm, perm_t, w_in, w_out, qpow, seg, drow, w_glu_b, bglu)
    out, st = pl.pallas_call(
        kern,
        grid=(s // tt,),
        in_specs=[pl.BlockSpec((tt, width), lambda t: (t, 0))] + [full(a) for a in consts],
        out_specs=[pl.BlockSpec((tt, width), lambda t: (t, 0)),
                   pl.BlockSpec((n_slab, 2 * S5_SLAB_STATE), lambda t: (0, 0))],
        out_shape=[jax.ShapeDtypeStruct((s, width), BF16),
                   jax.ShapeDtypeStruct((n_slab, 2 * S5_SLAB_STATE), F32)],
        scratch_shapes=[pltpu.VMEM((tt, 2 * S5_SLAB_STATE), F32),
                        pltpu.VMEM((tt, width), F32),
                        pltpu.VMEM((n_slab, 2 * S5_SLAB_STATE), F32)],
        compiler_params=_params(("arbitrary",)),
        name="s5_prompt",
    )(u, *consts)
    groups = n_slab * S5_SLAB_GROUPS
    s_re = st[:, :S5_SLAB_STATE].reshape(groups, S5_STATE)
    s_im = st[:, S5_SLAB_STATE:].reshape(groups, S5_STATE)
    return out, s_re, s_im


def _s5_sample_kernel(u_ref, sre_ref, sim_ref, win_ref, wout_ref, scan_ref, d_ref, wglu_ref,
                      bglu_ref, o_ref, xre_ref, xim_ref, y_scr):
    n_slab = win_ref.shape[0]
    ns = S5_SLAB_STATE
    for j in range(n_slab):
        lanes = slice(j * V7X_LANES, (j + 1) * V7X_LANES)
        st = slice(j * ns, (j + 1) * ns)
        u_j = u_ref[:, lanes]
        bu = jnp.dot(u_j.astype(BF16), win_ref[j], preferred_element_type=F32)
        ab_re, ab_im = scan_ref[j, 0, 0:1], scan_ref[j, 1, 0:1]
        s_re, s_im = sre_ref[:, st], sim_ref[:, st]
        xr = ab_re * s_re - ab_im * s_im + bu[:, 0:ns]
        xi = ab_re * s_im + ab_im * s_re + bu[:, ns:2 * ns]
        xre_ref[:, st] = xr
        xim_ref[:, st] = xi
        x = jnp.concatenate([xr, xi], axis=1).astype(BF16)
        y = jnp.dot(x, wout_ref[j], preferred_element_type=F32)
        y_scr[:, lanes] = y + d_ref[j] * u_j
    o_ref[...] = _glu(y_scr[...], wglu_ref, bglu_ref).astype(o_ref.dtype)


def _s5_sample(u, s_re, s_im, tabs, w_glu_b, b_glu):
    db, width = u.shape
    w_in, w_out, scan, dvec, _, _ = tabs
    n_state = s_re.shape[1] * s_re.shape[2]
    bglu = b_glu.reshape(1, width).astype(F32)
    args = (u, s_re.reshape(db, n_state).astype(F32), s_im.reshape(db, n_state).astype(F32),
            w_in, w_out, scan, dvec, w_glu_b, bglu)
    full = lambda a: pl.BlockSpec(a.shape, lambda i: (0,) * a.ndim)
    out, x_re, x_im = pl.pallas_call(
        _s5_sample_kernel,
        grid=(1,),
        in_specs=[full(a) for a in args],
        out_specs=[pl.BlockSpec((db, width), lambda i: (0, 0)),
                   pl.BlockSpec((db, n_state), lambda i: (0, 0)),
                   pl.BlockSpec((db, n_state), lambda i: (0, 0))],
        out_shape=[jax.ShapeDtypeStruct((db, width), BF16),
                   jax.ShapeDtypeStruct((db, n_state), F32),
                   jax.ShapeDtypeStruct((db, n_state), F32)],
        scratch_shapes=[pltpu.VMEM((db, width), F32)],
        compiler_params=_params(("arbitrary",)),
        name="s5_sample",
    )(*args)
    return out, x_re.reshape(s_re.shape), x_im.reshape(s_im.shape)


def _outproj_kernel(x_ref, a_ref, s_ref, w_ref, g_ref, h_ref, f_ref):
    half = a_ref.shape[1]
    h = (x_ref[...]
         + jnp.dot(a_ref[...], w_ref[0:half, :], preferred_element_type=F32)
         + jnp.dot(s_ref[...], w_ref[half:, :], preferred_element_type=F32))
    h_ref[...] = h
    ms = jnp.mean(h * h, axis=-1, keepdims=True)
    f_ref[...] = (h * lax.rsqrt(ms + EPS) * g_ref[...]).astype(BF16)


def _outproj(x, att, s5o, w_out_b, norm_ffn, tm):
    m, d = x.shape
    half = att.shape[1]
    tm = _tile(m, tm)
    row = lambda i: (i, 0)
    const = lambda i: (0, 0)
    return pl.pallas_call(
        _outproj_kernel,
        grid=(m // tm,),
        in_specs=[pl.BlockSpec((tm, d), row), pl.BlockSpec((tm, half), row),
                  pl.BlockSpec((tm, half), row), pl.BlockSpec((d, d), const),
                  pl.BlockSpec((1, d), const)],
        out_specs=[pl.BlockSpec((tm, d), row), pl.BlockSpec((tm, d), row)],
        out_shape=[jax.ShapeDtypeStruct((m, d), F32), jax.ShapeDtypeStruct((m, d), BF16)],
        compiler_params=_params(("parallel",)),
        name="out_proj",
    )(x, att, s5o, w_out_b, norm_ffn.reshape(1, d))


def _ffn_init(h_ref, o_ref):
    @pl.when(pl.program_id(1) == 0)
    def _():
        o_ref[...] = h_ref[...]


def _ffn_down(acts, wd_ref, o_ref):
    tc = acts[0].shape[1]
    part = jnp.dot(acts[0], wd_ref[0:tc, :], preferred_element_type=F32)
    for n in range(1, len(acts)):
        part = part + jnp.dot(acts[n], wd_ref[n * tc:(n + 1) * tc, :],
                              preferred_element_type=F32)
    o_ref[...] += part


def _ffn_prompt_kernel(f_ref, h_ref, wg_ref, wu_ref, wd_ref, cw_ref, cb_ref,
                       o_ref, buf_ref, hgx_ref, prev_ref, *, tm, tc):
    i = pl.program_id(0)
    j = pl.program_id(1)
    halo = V7X_SUBLANES
    tf = wg_ref.shape[1]
    _ffn_init(h_ref, o_ref)

    @pl.when(i == 0)
    def _():
        hgx_ref[0:halo, :] = jnp.zeros((halo, tf), F32)

    @pl.when(i > 0)
    def _():
        hgx_ref[0:halo, :] = prev_ref[j]

    f = f_ref[...]
    blocks = [slice(c0, c0 + tc) for c0 in range(0, tf, tc)]
    hgs = [None] * len(blocks)
    hus = [None] * len(blocks)

    def up(n):
        hgs[n] = jnp.dot(f, wg_ref[:, blocks[n]], preferred_element_type=F32)
        hus[n] = jnp.dot(f, wu_ref[:, blocks[n]], preferred_element_type=F32)

    up(0)
    part = None
    for n, cs in enumerate(blocks):
        if n + 1 < len(blocks):
            up(n + 1)
        hg = hgs[n]
        hgx_ref[halo:, cs] = hg
        cw = cw_ref[:, cs]
        conv = cb_ref[:, cs] + cw[2:3] * hg
        for tap in range(CONV_W - 1):
            off = halo - (CONV_W - 1) + tap
            conv = conv + cw[tap:tap + 1] * hgx_ref[pl.ds(off, tm), cs]
        act = (jax.nn.gelu(conv) * hus[n]).astype(BF16)
        down = jnp.dot(act, wd_ref[cs, :], preferred_element_type=F32)
        part = down if part is None else part + down
    tail = hgx_ref[tm:tm + halo, :]
    prev_ref[j] = tail
    buf_ref[...] = tail
    o_ref[...] += part


def _ffn_sample_kernel(f_ref, h_ref, wg_ref, wu_ref, wd_ref, cw_ref, cb_ref, b0_ref, b1_ref,
                       o_ref, hg_ref):
    _ffn_init(h_ref, o_ref)
    f = f_ref[...]
    hg = jnp.dot(f, wg_ref[...], preferred_element_type=F32)
    hu = jnp.dot(f, wu_ref[...], preferred_element_type=F32)
    hg_ref[...] = hg
    cw = cw_ref[...]
    conv = cb_ref[...] + cw[0:1] * b0_ref[...] + cw[1:2] * b1_ref[...] + cw[2:3] * hg
    _ffn_down([(jax.nn.gelu(conv) * hu).astype(BF16)], wd_ref, o_ref)


def _ffn(f, h, w_gate_b, w_up_b, w_down_b, conv_w, conv_b, conv_buf, tm, tf):
    m, d = f.shape
    dff = w_gate_b.shape[1]
    tm = _tile(m, tm)
    tf = _tile(dff, tf)
    n_ff = dff // tf
    row = lambda i, j: (i, 0)
    col = lambda i, j: (0, j)
    common_in = [pl.BlockSpec((tm, d), row), pl.BlockSpec((tm, d), row),
                 pl.BlockSpec((d, tf), col), pl.BlockSpec((d, tf), col),
                 pl.BlockSpec((tf, d), lambda i, j: (j, 0)),
                 pl.BlockSpec((CONV_W, tf), col), pl.BlockSpec((1, tf), col)]
    common_args = (f, h, w_gate_b, w_up_b, w_down_b, conv_w.astype(F32),
                   conv_b.reshape(1, dff).astype(F32))
    if conv_buf is None:
        halo = V7X_SUBLANES
        kern = functools.partial(_ffn_prompt_kernel, tm=tm, tc=_tile(tf, V7X_MXU_COLS))
        out, tail = pl.pallas_call(
            kern,
            grid=(m // tm, n_ff),
            in_specs=common_in,
            out_specs=[pl.BlockSpec((tm, d), row),
                       pl.BlockSpec((None, halo, tf), lambda i, j: (i, 0, j))],
            out_shape=[jax.ShapeDtypeStruct((m, d), F32),
                       jax.ShapeDtypeStruct((m // tm, halo, dff), F32)],
            scratch_shapes=[pltpu.VMEM((tm + halo, tf), F32),
                            pltpu.VMEM((n_ff, halo, tf), F32)],
            compiler_params=_params(("arbitrary", "arbitrary")),
            name="ffn_prompt",
        )(*common_args)
        return out, tail[-1, halo - (CONV_W - 1):]
    b0 = conv_buf[:, 0, :].astype(F32)
    b1 = conv_buf[:, 1, :].astype(F32)
    out, hg = pl.pallas_call(
        _ffn_sample_kernel,
        grid=(m // tm, n_ff),
        in_specs=common_in + [pl.BlockSpec((tm, tf), lambda i, j: (i, j))] * 2,
        out_specs=[pl.BlockSpec((tm, d), row), pl.BlockSpec((tm, tf), lambda i, j: (i, j))],
        out_shape=[jax.ShapeDtypeStruct((m, d), F32), jax.ShapeDtypeStruct((m, dff), F32)],
        compiler_params=_params(("parallel", "arbitrary")),
        name="ffn_sample",
    )(*common_args, b0, b1)
    return out, jnp.stack([b1, hg], axis=1)


def _ple_kernel(h_ref, p_ref, wg_ref, wp_ref, g_ref, o_ref):
    h = h_ref[...]
    gate = jax.nn.sigmoid(jnp.dot(h.astype(BF16), wg_ref[...], preferred_element_type=F32))
    e = jnp.dot(p_ref[...].astype(BF16), wp_ref[...], preferred_element_type=F32)
    ms = jnp.mean(e * e, axis=-1, keepdims=True)
    o_ref[...] = h + gate * (e * lax.rsqrt(ms + EPS) * g_ref[...])


def _ple(h, p, w_gate_b, w_proj_b, ple_norm, tm):
    m, d = h.shape
    pd = p.shape[1]
    tm = _tile(m, tm)
    row = lambda i: (i, 0)
    const = lambda i: (0, 0)
    return pl.pallas_call(
        _ple_kernel,
        grid=(m // tm,),
        in_specs=[pl.BlockSpec((tm, d), row), pl.BlockSpec((tm, pd), row),
                  pl.BlockSpec((d, d), const), pl.BlockSpec((pd, d), const),
                  pl.BlockSpec((1, d), const)],
        out_specs=pl.BlockSpec((tm, d), row),
        out_shape=jax.ShapeDtypeStruct((m, d), F32),
        compiler_params=_params(("parallel",)),
        name="ple_gate",
    )(h, p, w_gate_b, w_proj_b, ple_norm.reshape(1, d))


def kernel(x_prompt, x_sample, cache_k, cache_v, state_s5_re, state_s5_im, state_conv, page_table, p_prompt, p_sample, norm_mix, w_in, q_norm, k_norm, lam_q1, lam_k1, lam_q2, lam_k2, subln, s5_a_re, s5_a_im, s5_b_re, s5_b_im, s5_c_re, s5_c_im, s5_d, s5_log_dt, w_glu, b_glu, w_out, norm_ffn, w_gate, w_up, conv_w, conv_b, w_down, w_ple_gate, w_ple_proj, ple_norm):
    depth = w_in.shape[0]
    b, s, d = x_prompt.shape
    db, t_new, _ = x_sample.shape
    assert b == 1 and t_new == 1
    page = cache_k.shape[2]
    past_len = page_table.shape[1] * page
    hp = x_prompt.reshape(s, d)
    hs = x_sample.reshape(db, d)
    pos_p = np.arange(s)
    pos_s = np.full((db,), past_len)
    n_pool = cache_k.shape[1]
    ck = jnp.transpose(cache_k, (0, 1, 3, 4, 5, 2)).reshape(depth * n_pool, -1, page)
    cv = cache_v.reshape(depth * n_pool, page * N_HEADS, V_DIM)
    outs = [[] for _ in range(10)]
    for i in range(depth):
        lam_init = 0.8 - 0.6 * math.exp(-0.3 * i)
        lam_vec = jnp.stack([lam_q1[i], lam_k1[i], lam_q2[i], lam_k2[i]]).astype(F32)
        w_in_b = w_in[i].astype(BF16)
        w_glu_b = w_glu[i].astype(BF16)
        w_out_b = w_out[i].astype(BF16)
        w_gate_b = w_gate[i].astype(BF16)
        w_up_b = w_up[i].astype(BF16)
        w_down_b = w_down[i].astype(BF16)
        w_pg_b = w_ple_gate[i].astype(BF16)
        w_pp_b = w_ple_proj[i].astype(BF16)
        tabs = _s5_tables(s5_a_re[i], s5_a_im[i], s5_b_re[i], s5_b_im[i], s5_c_re[i],
                          s5_c_im[i], s5_d[i], s5_log_dt[i], S5_TILE // V7X_SUBLANES)

        q, k32, kb, v32, vt, u = _project(hp, pos_p, norm_mix[i], w_in_b, q_norm[i], k_norm[i],
                                          ROW_TILE, True)
        qs, k32s, kbs, v32s, vbs, us = _project(hs, pos_s, norm_mix[i], w_in_b, q_norm[i],
                                                k_norm[i], ROW_TILE, False)
        att, att_s = _attention(q, kb, vt, qs, kbs, vbs, ck, cv, page_table + i * n_pool,
                                lam_vec, subln[i], lam_init, ATTN_TILE)

        s5o, s_re, s_im = _s5_prompt(u, tabs, w_glu_b, b_glu[i], S5_TILE)
        h1, f = _outproj(hp, att, s5o, w_out_b, norm_ffn[i], ROW_TILE)
        h2, cbuf = _ffn(f, h1, w_gate_b, w_up_b, w_down_b, conv_w[i], conv_b[i], None,
                        ROW_TILE, FFN_COL_TILE)
        hp = _ple(h2, p_prompt[i].reshape(s, -1), w_pg_b, w_pp_b, ple_norm[i], ROW_TILE)
        outs[0].append(k32.reshape(b, s, N_HEADS, 2, HEAD_DIM))
        outs[1].append(v32.reshape(b, s, N_HEADS, V_DIM))
        outs[2].append(s_re[None])
        outs[3].append(s_im[None])
        outs[4].append(cbuf[None])

        s5o, s_re, s_im = _s5_sample(us, state_s5_re[i], state_s5_im[i], tabs, w_glu_b, b_glu[i])
        h1, f = _outproj(hs, att_s, s5o, w_out_b, norm_ffn[i], ROW_TILE)
        h2, cbuf = _ffn(f, h1, w_gate_b, w_up_b, w_down_b, conv_w[i], conv_b[i], state_conv[i],
                        ROW_TILE, FFN_COL_TILE)
        hs = _ple(h2, p_sample[i].reshape(db, -1), w_pg_b, w_pp_b, ple_norm[i], ROW_TILE)
        outs[5].append(k32s.reshape(db, t_new, N_HEADS, 2, HEAD_DIM))
        outs[6].append(v32s.reshape(db, t_new, N_HEADS, V_DIM))
        outs[7].append(s_re)
        outs[8].append(s_im)
        outs[9].append(cbuf)
    return (hp.reshape(b, s, d), hs.reshape(db, t_new, d)) + tuple(jnp.stack(o) for o in outs)
```

```python
import functools
import math

import jax
import jax.numpy as jnp
import numpy as np
from jax import lax
from jax.experimental import pallas as pl
from jax.experimental.pallas import tpu as pltpu

N_HEADS = 8
HEAD_DIM = 64
V_DIM = 2 * HEAD_DIM
ROT_DIM = HEAD_DIM // 4
ROPE_THETA = 500000.0
S5_CH = 16
S5_STATE = 64
CONV_W = 3
EPS = 1e-6
Q_SCALE = math.log2(math.e) * HEAD_DIM ** -0.5

V7X_LANES = 128
V7X_SUBLANES = 8
V7X_MXU_COLS = 256
VMEM_LIMIT_BYTES = 56 * 1024 * 1024
ATTN_VMEM_LIMIT_BYTES = 60 * 1024 * 1024

S5_SLAB_GROUPS = V7X_LANES // S5_CH
S5_SLAB_STATE = S5_SLAB_GROUPS * S5_STATE

ROW_TILE = 512
ATTN_TILE = 1024
FFN_COL_TILE = 512
S5_TILE = 512

F32 = jnp.float32
BF16 = jnp.bfloat16


def _tile(n, pref):
    t = min(n, pref)
    assert n % t == 0, (n, t)
    return t


def _params(sem, vmem=VMEM_LIMIT_BYTES):
    return pltpu.CompilerParams(dimension_semantics=sem, vmem_limit_bytes=vmem)


def _qk_post(p, gain, cos, sin_lo, sin_hi, seg):
    outs = []
    for c in range(p.shape[1] // V7X_LANES):
        pc = p[:, c * V7X_LANES:(c + 1) * V7X_LANES]
        ms = jnp.dot((pc * pc).astype(BF16), seg, preferred_element_type=F32)
        y = pc * lax.rsqrt(ms + EPS) * gain
        y = (y * cos
             + pltpu.roll(y, V7X_LANES - ROT_DIM // 2, 1) * sin_lo
             + pltpu.roll(y, ROT_DIM // 2, 1) * sin_hi)
        outs.append(y)
    return jnp.concatenate(outs, axis=1)


def _proj_kernel(x_ref, g_ref, w_ref, qg_ref, kg_ref, cos_ref, slo_ref, shi_ref, seg_ref,
                 q_ref, k32_ref, kb_ref, v32_ref, vb_ref, u_ref, *, v_transposed, bw):
    width = q_ref.shape[1]
    nb = width // bw
    x = x_ref[...]
    ms = jnp.mean(x * x, axis=-1, keepdims=True)
    xn = (x * lax.rsqrt(ms + EPS) * g_ref[...]).astype(BF16)
    rope = (cos_ref[...], slo_ref[...], shi_ref[...], seg_ref[...])

    def block(c):
        return jnp.dot(xn, w_ref[:, c * bw:(c + 1) * bw], preferred_element_type=F32)

    for b in range(nb):
        cols = slice(b * bw, (b + 1) * bw)
        q = _qk_post(block(b), qg_ref[...], *rope)
        q_ref[:, cols] = (q * Q_SCALE).astype(BF16)
    for b in range(nb):
        cols = slice(b * bw, (b + 1) * bw)
        k = _qk_post(block(nb + b), kg_ref[...], *rope)
        k32_ref[:, cols] = k
        kb_ref[:, cols] = k.astype(BF16)
    for b in range(nb):
        cols = slice(b * bw, (b + 1) * bw)
        v = block(2 * nb + b)
        v32_ref[:, cols] = v
        if v_transposed:
            for hh in range(bw // V_DIM):
                vb_ref[b * (bw // V_DIM) + hh, 0] = (
                    v[:, hh * V_DIM:(hh + 1) * V_DIM].T.astype(BF16))
        else:
            vb_ref[:, cols] = v.astype(BF16)
    for b in range(nb):
        u_ref[:, b * bw:(b + 1) * bw] = block(3 * nb + b)


def _rope_tables(pos):
    half = ROT_DIM // 2
    inv = ROPE_THETA ** (-(np.arange(half, dtype=np.float64) * 2.0 / ROT_DIM))
    ang = np.asarray(pos, np.float64)[:, None] * inv[None, :]
    cos, sin = np.cos(ang), np.sin(ang)
    t = ang.shape[0]
    ones = np.ones((t, HEAD_DIM - ROT_DIM))
    zeros = np.zeros((t, HEAD_DIM - ROT_DIM))
    zh = np.zeros((t, half))
    cos_t = np.concatenate([cos, cos, ones], axis=1)
    lo_t = np.concatenate([-sin, zh, zeros], axis=1)
    hi_t = np.concatenate([zh, sin, zeros], axis=1)
    rep = V7X_LANES // HEAD_DIM
    return tuple(jnp.asarray(np.tile(tab, (1, rep)), F32) for tab in (cos_t, lo_t, hi_t))


def _project(x, pos, norm_mix, w_in_b, q_norm, k_norm, tm, v_transposed):
    m, d = x.shape
    width = w_in_b.shape[1] // 4
    tm = _tile(m, tm)
    if v_transposed:
        vb_spec = pl.BlockSpec((N_HEADS, 1, V_DIM, tm), lambda i: (0, i, 0, 0))
        vb_shape = jax.ShapeDtypeStruct((N_HEADS, m // tm, V_DIM, tm), BF16)
    else:
        vb_spec = pl.BlockSpec((tm, width), lambda i: (i, 0))
        vb_shape = jax.ShapeDtypeStruct((m, width), BF16)
    cos_t, lo_t, hi_t = _rope_tables(pos)
    rep = V7X_LANES // HEAD_DIM
    qg = jnp.tile(q_norm.reshape(1, HEAD_DIM), (1, rep))
    kg = jnp.tile(k_norm.reshape(1, HEAD_DIM), (1, rep))
    lane = np.arange(V7X_LANES)
    seg = jnp.asarray((lane[:, None] // HEAD_DIM == lane[None, :] // HEAD_DIM) / HEAD_DIM, BF16)
    row = lambda i: (i, 0)
    const = lambda i: (0, 0)
    out_spec = pl.BlockSpec((tm, width), row)
    kern = functools.partial(_proj_kernel, v_transposed=v_transposed,
                             bw=_tile(width, 2 * V7X_MXU_COLS))
    return pl.pallas_call(
        kern,
        grid=(m // tm,),
        in_specs=[
            pl.BlockSpec((tm, d), row),
            pl.BlockSpec((1, d), const),
            pl.BlockSpec((d, 4 * width), const, pipeline_mode=pl.Buffered(1)),
            pl.BlockSpec((1, V7X_LANES), const),
            pl.BlockSpec((1, V7X_LANES), const),
            pl.BlockSpec((tm, V7X_LANES), row),
            pl.BlockSpec((tm, V7X_LANES), row),
            pl.BlockSpec((tm, V7X_LANES), row),
            pl.BlockSpec((V7X_LANES, V7X_LANES), const),
        ],
        out_specs=[out_spec] * 4 + [vb_spec, out_spec],
        out_shape=[
            jax.ShapeDtypeStruct((m, width), BF16),
            jax.ShapeDtypeStruct((m, width), F32),
            jax.ShapeDtypeStruct((m, width), BF16),
            jax.ShapeDtypeStruct((m, width), F32),
            vb_shape,
            jax.ShapeDtypeStruct((m, width), F32),
        ],
        compiler_params=_params(("parallel",)),
        name="in_proj",
    )(x, norm_mix.reshape(1, d), w_in_b, qg, kg, cos_t, lo_t, hi_t, seg)


def _lambda(lam_ref, lam_init):
    lv = lam_ref[...]
    l1 = jnp.sum(lv[0:1] * lv[1:2], axis=-1, keepdims=True)
    l2 = jnp.sum(lv[2:3] * lv[3:4], axis=-1, keepdims=True)
    return jnp.exp(l1) - jnp.exp(l2) + lam_init


def _subln(out, sub, lam_init):
    ms = jnp.mean(out * out, axis=-1, keepdims=True)
    return out * lax.rsqrt(ms + EPS) * sub * (1.0 - lam_init)


def _prompt_chunk(q_ref, k_ref, vt_ref, acc_ref, st_ref, ml_ref, qi, kc, *, tq, masked):
    tv = vt_ref.shape[-1]
    n_sub = tq // tv
    q = q_ref[...].astype(F32)
    lane = lax.broadcasted_iota(jnp.int32, q.shape, 1)
    qs = jnp.concatenate([jnp.where(lane < HEAD_DIM, q, 0.0),
                          jnp.where(lane >= HEAD_DIM, q, 0.0)], axis=0).astype(BF16)

    def scores(j):
        k0 = pl.multiple_of(kc * tq + j * tv, tv)
        st_ref[j] = lax.dot_general(k_ref[pl.ds(k0, tv), :], qs, (((1,), (1,)), ((), ())),
                                    preferred_element_type=F32)

    m, l = ml_ref[0:1, :], ml_ref[1:2, :]
    scores(0)
    for j in range(n_sub):
        if j + 1 < n_sub:
            scores(j + 1)
        st = st_ref[j]
        if masked:
            key = lax.broadcasted_iota(jnp.int32, st.shape, 0) + j * tv
            qry = lax.broadcasted_iota(jnp.int32, st.shape, 1)
            qry = jnp.where(qry >= tq, qry - tq, qry)
            st = jnp.where(key <= qry, st, -jnp.inf)
        m_new = jnp.maximum(m, jnp.max(st, axis=0, keepdims=True))
        alpha = jnp.exp2(m - m_new)
        p = jnp.exp2(st - m_new)
        l = alpha * l + jnp.sum(p, axis=0, keepdims=True)
        m = m_new
        pv = jnp.dot(vt_ref[kc * n_sub + j], p.astype(BF16), preferred_element_type=F32)
        acc_ref[...] = alpha * acc_ref[...] + pv
    ml_ref[0:1, :] = m
    ml_ref[1:2, :] = l


def _prompt_finish(acc_ref, ml_ref, lam_ref, sub_ref, o_ref, *, tq, lam_init):
    lam = _lambda(lam_ref, lam_init)
    o = acc_ref[...] / ml_ref[1:2, :]
    out = o[:, :tq] - lam * o[:, tq:]
    ms = jnp.mean(out * out, axis=0, keepdims=True)
    y = out * lax.rsqrt(ms + EPS) * sub_ref[...] * (1.0 - lam_init)
    o_ref[...] = y.T.astype(o_ref.dtype)


def _sample_init(q_ref, qm_ref, m_ref, l_ref, acc_ref):
    n_rows, width = qm_ref.shape
    q = jnp.broadcast_to(q_ref[0].astype(F32), (n_rows, width))
    r = lax.broadcasted_iota(jnp.int32, (n_rows, width), 0)
    c = lax.broadcasted_iota(jnp.int32, (n_rows, width), 1)
    keep = (c >= r * HEAD_DIM) & (c < (r + 1) * HEAD_DIM)
    qm_ref[...] = jnp.where(keep, q, 0.0).astype(BF16)
    m_ref[...] = jnp.full(m_ref.shape, -jnp.inf, F32)
    l_ref[...] = jnp.zeros(l_ref.shape, F32)
    acc_ref[...] = jnp.zeros(acc_ref.shape, F32)


def _sample_pages(k_refs, v_refs, exp_ref, qm_ref, m_ref, l_ref, acc_ref):
    n_pg = len(k_refs)
    n_rows = 2 * N_HEADS
    page = k_refs[0].shape[1]
    qm = qm_ref[...]
    s = jnp.concatenate(
        [jnp.dot(qm, k_refs[g][...].astype(BF16), preferred_element_type=F32)
         for g in range(n_pg)], axis=1)
    m_prev = m_ref[...]
    m_new = jnp.maximum(m_prev, jnp.max(s, axis=1, keepdims=True))
    alpha = jnp.exp2(m_prev - m_new)
    p = jnp.exp2(s - m_new)
    l_new = alpha * l_ref[...] + jnp.sum(p, axis=1, keepdims=True)
    pb = p.astype(BF16)
    p_rows = jnp.concatenate([pb[:, g * page:(g + 1) * page] for g in range(n_pg)], axis=0)
    p_wide = jnp.dot(p_rows, exp_ref[...], preferred_element_type=F32)
    r = lax.broadcasted_iota(jnp.int32, p_wide.shape, 0)
    c = lax.broadcasted_iota(jnp.int32, p_wide.shape, 1)
    own = (c % N_HEADS) == ((r % n_rows) // 2)
    p_wide = jnp.where(own, p_wide, 0.0).astype(BF16)
    acc = alpha * acc_ref[...]
    for g in range(n_pg):
        acc = acc + jnp.dot(p_wide[g * n_rows:(g + 1) * n_rows], v_refs[g][...].astype(BF16),
                            preferred_element_type=F32)
    m_ref[...] = m_new
    l_ref[...] = l_new
    acc_ref[...] = acc


def _sample_finish(kn_ref, vn_ref, lam_ref, sub_ref, o_ref, qm_ref, m_ref, l_ref, acc_ref,
                   *, lam_init):
    kn = kn_ref[0].astype(F32)
    vn = vn_ref[0].astype(F32)
    s_new = jnp.sum(qm_ref[...].astype(F32) * kn, axis=1, keepdims=True)
    m_old = m_ref[...]
    m_fin = jnp.maximum(m_old, s_new)
    a = jnp.exp2(m_old - m_fin)
    p_new = jnp.exp2(s_new - m_fin)
    l_fin = a * l_ref[...] + p_new
    acc_fin = a * acc_ref[...] + p_new.astype(BF16).astype(F32) * vn
    o = acc_fin / l_fin
    o1 = jnp.concatenate([o[2 * h:2 * h + 1] for h in range(N_HEADS)], axis=0)
    o2 = jnp.concatenate([o[2 * h + 1:2 * h + 2] for h in range(N_HEADS)], axis=0)
    lam = _lambda(lam_ref, lam_init)
    o_ref[0] = _subln(o1 - lam * o2, sub_ref[...], lam_init)


def _attn_kernel(sh_ref, sq_ref, sk_ref, pt_ref, *refs, n_pg, tq, lam_init, n_sample_steps,
                 steps_per_seq):
    del sh_ref, pt_ref
    k_pages = refs[:n_pg]
    v_pages = refs[n_pg:2 * n_pg]
    (q_ref, k_ref, vt_ref, lam_ref, subc_ref, qs_ref, kn_ref, vn_ref, exp_ref, subr_ref,
     o_ref, os_ref, acc_ref, st_ref, ml_ref, qm_ref, ms_ref, ls_ref, accs_ref) = refs[2 * n_pg:]
    t = pl.program_id(0)
    qi = sq_ref[t]
    kc = sk_ref[t]
    valid = t < n_sample_steps
    s_id = t % steps_per_seq

    @pl.when(kc == 0)
    def _():
        acc_ref[...] = jnp.zeros(acc_ref.shape, F32)
        ml_ref[0:1, :] = jnp.full((1, ml_ref.shape[1]), -jnp.inf, F32)
        ml_ref[1:2, :] = jnp.zeros((1, ml_ref.shape[1]), F32)

    @pl.when(valid & (s_id == 0))
    def _():
        _sample_init(qs_ref, qm_ref, ms_ref, ls_ref, accs_ref)

    chunk = functools.partial(_prompt_chunk, q_ref, k_ref, vt_ref, acc_ref, st_ref, ml_ref,
                              qi, kc, tq=tq)

    @pl.when(kc < qi)
    def _():
        chunk(masked=False)

    @pl.when(kc == qi)
    def _():
        chunk(masked=True)
        _prompt_finish(acc_ref, ml_ref, lam_ref, subc_ref, o_ref, tq=tq, lam_init=lam_init)

    @pl.when(valid)
    def _():
        _sample_pages(k_pages, v_pages, exp_ref, qm_ref, ms_ref, ls_ref, accs_ref)

    @pl.when(valid & (s_id == steps_per_seq - 1))
    def _():
        _sample_finish(kn_ref, vn_ref, lam_ref, subr_ref, os_ref, qm_ref, ms_ref, ls_ref,
                       accs_ref, lam_init=lam_init)


def _attention(q, k, vt, qs, k_new, v_new, ck, cv, page_idx, lam_vec, subln, lam_init, tq):
    s, width = q.shape
    db = qs.shape[0]
    tv = vt.shape[-1]
    tq = _tile(s, max(tq, tv))
    assert tq % tv == 0
    nq = s // tq
    page = ck.shape[2]
    n_pages = page_idx.shape[1]
    n_rows = 2 * N_HEADS

    sched = [(h, qi, kc) for h in range(N_HEADS) for qi in range(nq) for kc in range(qi + 1)]
    n_steps = len(sched)
    sh, sq, sk = (jnp.asarray([e[n] for e in sched], jnp.int32) for n in range(3))
    n_pg = next(g for g in range(1, n_pages + 1)
                if n_pages % g == 0 and db * (n_pages // g) <= n_steps)
    steps_per_seq = n_pages // n_pg
    n_sample_steps = db * steps_per_seq

    tok = np.arange(page)
    expand = jnp.asarray(tok[:, None] == (np.arange(page * N_HEADS)[None, :] // N_HEADS), BF16)
    vn = jnp.repeat(v_new.reshape(db, N_HEADS, V_DIM), 2, axis=1)

    def seq_of(t):
        ts = jnp.minimum(t, n_sample_steps - 1)
        return ts // steps_per_seq, ts % steps_per_seq

    def page_spec(shape, g):
        def index(t, sh, sq, sk, pt):
            b, s_id = seq_of(t)
            return (pt[b, s_id * n_pg + g], 0, 0)
        return pl.BlockSpec((None,) + shape, index)

    const = lambda t, sh, sq, sk, pt: (0, 0)
    tile = lambda t, sh, sq, sk, pt: (sq[t], sh[t])
    seq = lambda t, sh, sq, sk, pt: (seq_of(t)[0], 0, 0)
    kern = functools.partial(_attn_kernel, n_pg=n_pg, tq=tq, lam_init=lam_init,
                             n_sample_steps=n_sample_steps, steps_per_seq=steps_per_seq)
    att, att_s = pl.pallas_call(
        kern,
        grid_spec=pltpu.PrefetchScalarGridSpec(
            num_scalar_prefetch=4,
            grid=(n_steps,),
            in_specs=([page_spec((width, page), g) for g in range(n_pg)]
                      + [page_spec((page * N_HEADS, V_DIM), g) for g in range(n_pg)]
                      + [pl.BlockSpec((tq, V_DIM), tile),
                         pl.BlockSpec((s, V_DIM), lambda t, sh, sq, sk, pt: (0, sh[t])),
                         pl.BlockSpec((None, s // tv, V_DIM, tv),
                                      lambda t, sh, sq, sk, pt: (sh[t], 0, 0, 0)),
                         pl.BlockSpec((4, HEAD_DIM), const),
                         pl.BlockSpec((V_DIM, 1), const),
                         pl.BlockSpec((1, 1, width), seq), pl.BlockSpec((1, 1, width), seq),
                         pl.BlockSpec((1, n_rows, V_DIM), seq),
                         pl.BlockSpec(expand.shape, const),
                         pl.BlockSpec((1, V_DIM), const)]),
            out_specs=[pl.BlockSpec((tq, V_DIM), tile),
                       pl.BlockSpec((1, N_HEADS, V_DIM), seq)],
            scratch_shapes=[pltpu.VMEM((V_DIM, 2 * tq), F32),
                            pltpu.VMEM((tq // tv, tv, 2 * tq), F32),
                            pltpu.VMEM((2, 2 * tq), F32),
                            pltpu.VMEM((n_rows, width), BF16),
                            pltpu.VMEM((n_rows, 1), F32), pltpu.VMEM((n_rows, 1), F32),
                            pltpu.VMEM((n_rows, V_DIM), F32)],
        ),
        out_shape=[jax.ShapeDtypeStruct((s, width), BF16),
                   jax.ShapeDtypeStruct((db, N_HEADS, V_DIM), F32)],
        compiler_params=_params(("arbitrary",), ATTN_VMEM_LIMIT_BYTES),
        name="attention",
    )(sh, sq, sk, page_idx, *([ck] * n_pg), *([cv] * n_pg),
      q, k, vt, lam_vec, subln.reshape(V_DIM, 1),
      qs.reshape(db, 1, width), k_new.reshape(db, 1, width), vn, expand,
      subln.reshape(1, V_DIM))
    return att, att_s.reshape(db, width).astype(BF16)


def _s5_tables(a_re, a_im, b_re, b_im, c_re, c_im, d, log_dt, seg_len):
    g, p = a_re.shape
    n_slab = g // S5_SLAB_GROUPS
    dt = jnp.exp(log_dt.astype(F32))[:, None]
    mag = jnp.exp(dt * a_re)
    ab_re = mag * jnp.cos(dt * a_im)
    ab_im = mag * jnp.sin(dt * a_im)
    den = a_re * a_re + a_im * a_im
    n_re = ab_re - 1.0
    f_re = (n_re * a_re + ab_im * a_im) / den
    f_im = (ab_im * a_re - n_re * a_im) / den
    bb_re = f_re[..., None] * b_re - f_im[..., None] * b_im
    bb_im = f_re[..., None] * b_im + f_im[..., None] * b_re
    eye = jnp.eye(S5_SLAB_GROUPS, dtype=F32)

    def in_mat(bb):
        bb = bb.reshape(n_slab, S5_SLAB_GROUPS, p, S5_CH)
        return jnp.einsum('jgpc,gh->jgchp', bb, eye).reshape(n_slab, V7X_LANES, S5_SLAB_STATE)

    def out_mat(cc):
        cc = cc.reshape(n_slab, S5_SLAB_GROUPS, S5_CH, p)
        return jnp.einsum('jgcp,gh->jgphc', cc, eye).reshape(n_slab, S5_SLAB_STATE, V7X_LANES)

    w_in = jnp.concatenate([in_mat(bb_re), in_mat(bb_im)], axis=2).astype(BF16)
    w_out = jnp.concatenate([out_mat(c_re.astype(F32)), out_mat(-c_im.astype(F32))],
                            axis=1).astype(BF16)

    def apow(ns):
        ns = jnp.asarray(ns, F32)[:, None, None]
        mag = jnp.exp(ns * (dt * a_re))
        ang = ns * (dt * a_im)
        flat = lambda t: jnp.swapaxes(t.reshape(-1, n_slab, S5_SLAB_STATE), 0, 1)
        return flat(mag * jnp.cos(ang)), flat(mag * jnp.sin(ang))

    def block_scan_tables(step):
        p_re, p_im = apow(step * np.arange(1, V7X_SUBLANES + 1))
        rows = np.arange(V7X_SUBLANES)[None, :, None]
        tabs = [p_re, p_im]
        for k in (1, 2, 4):
            tabs.append(jnp.where(rows >= k, p_re[:, k - 1:k], 0.0))
            tabs.append(jnp.where(rows >= k, p_im[:, k - 1:k], 0.0))
        return jnp.stack(tabs, axis=1)

    scan = block_scan_tables(1)
    qpow = jnp.stack(apow(np.arange(1, seg_len + 1)), axis=1)
    seg = block_scan_tables(seg_len)
    dvec = d.astype(F32).reshape(n_slab, 1, V7X_LANES)
    return w_in, w_out, scan, dvec, qpow, seg


def _glu(y, wglu_ref, bglu_ref):
    g = jax.nn.gelu(y)
    z = jnp.dot(g.astype(BF16), wglu_ref[...], preferred_element_type=F32) + bglu_ref[...]
    return g * jax.nn.sigmoid(z)


def _s5_prompt_kernel(u_ref, perm_ref, permt_ref, win_ref, wout_ref, qpow_ref, seg_ref, d_ref,
                      wglu_ref, bglu_ref, o_ref, st_ref, x_scr, y_scr, c_scr, *, tt):
    t = pl.program_id(0)
    n_slab = win_ref.shape[0]
    ns = S5_SLAB_STATE
    sub = V7X_SUBLANES
    seg_len = tt // sub

    @pl.when(t == 0)
    def _():
        c_scr[...] = jnp.zeros(c_scr.shape, F32)

    u = u_ref[...]
    ub = jnp.dot(perm_ref[...], u.astype(BF16), preferred_element_type=F32).astype(BF16)
    row = lax.broadcasted_iota(jnp.int32, (sub, ns), 0)

    for j in range(n_slab):
        lanes = slice(j * V7X_LANES, (j + 1) * V7X_LANES)
        x_scr[...] = jnp.dot(ub[:, lanes], win_ref[j], preferred_element_type=F32)
        a_re, a_im = qpow_ref[j, 0, 0:1, :], qpow_ref[j, 1, 0:1, :]

        xr, xi = x_scr[0:sub, 0:ns], x_scr[0:sub, ns:2 * ns]
        for i in range(1, seg_len):
            rows = slice(i * sub, (i + 1) * sub)
            xr, xi = (a_re * xr - a_im * xi + x_scr[rows, 0:ns],
                      a_re * xi + a_im * xr + x_scr[rows, ns:2 * ns])
            x_scr[rows, 0:ns] = xr
            x_scr[rows, ns:2 * ns] = xi

        cr, ci = c_scr[j:j + 1, 0:ns], c_scr[j:j + 1, ns:2 * ns]
        er, ei = xr, xi
        for n, k in enumerate((1, 2, 4)):
            mr, mi = seg_ref[j, 2 + 2 * n], seg_ref[j, 3 + 2 * n]
            sr, si = pltpu.roll(er, k, 0), pltpu.roll(ei, k, 0)
            er, ei = er + mr * sr - mi * si, ei + mr * si + mi * sr
        p_re, p_im = seg_ref[j, 0], seg_ref[j, 1]
        er, ei = er + p_re * cr - p_im * ci, ei + p_re * ci + p_im * cr
        c_scr[j:j + 1, 0:ns] = er[sub - 1:sub]
        c_scr[j:j + 1, ns:2 * ns] = ei[sub - 1:sub]
        sr = jnp.where(row == 0, cr, pltpu.roll(er, 1, 0))
        si = jnp.where(row == 0, ci, pltpu.roll(ei, 1, 0))

        for i in range(seg_len):
            rows = slice(i * sub, (i + 1) * sub)
            q_re, q_im = qpow_ref[j, 0, i:i + 1, :], qpow_ref[j, 1, i:i + 1, :]
            x_scr[rows, 0:ns] += q_re * sr - q_im * si
            x_scr[rows, ns:2 * ns] += q_re * si + q_im * sr
        y_scr[:, lanes] = jnp.dot(x_scr[...].astype(BF16), wout_ref[j],
                                  preferred_element_type=F32)

    yp = y_scr[...]
    hi = yp.astype(BF16)
    rest = yp - hi.astype(F32)
    mid = rest.astype(BF16)
    low = (rest - mid.astype(F32)).astype(BF16)
    permt = permt_ref[...]
    y = (jnp.dot(permt, hi, preferred_element_type=F32)
         + jnp.dot(permt, mid, preferred_element_type=F32)
         + jnp.dot(permt, low, preferred_element_type=F32))
    y = y + d_ref[...] * u
    o_ref[...] = _glu(y, wglu_ref, bglu_ref).astype(o_ref.dtype)
    st_ref[...] = c_scr[...]


def _s5_prompt(u, tabs, w_glu_b, b_glu, tt):
    s, width = u.shape
    w_in, w_out, _, dvec, qpow, seg = tabs
    n_slab = w_in.shape[0]
    tt = _tile(s, tt)
    assert qpow.shape[2] * V7X_SUBLANES == tt
    seg_len = tt // V7X_SUBLANES
    src = (np.arange(tt) % V7X_SUBLANES) * seg_len + np.arange(tt) // V7X_SUBLANES
    perm_np = src[:, None] == np.arange(tt)[None, :]
    perm, perm_t = jnp.asarray(perm_np, BF16), jnp.asarray(perm_np.T, BF16)
    full = lambda a: pl.BlockSpec(a.shape, lambda t: (0,) * a.ndim)
    bglu = b_glu.reshape(1, width).astype(F32)
    drow = dvec.reshape(1, width)
    kern = functools.partial(_s5_prompt_kernel, tt=tt)
    consts = (perm, perm_t, w_in, w_out, qpow, seg, drow, w_glu_b, bglu)
    out, st = pl.pallas_call(
        kern,
        grid=(s // tt,),
        in_specs=[pl.BlockSpec((tt, width), lambda t: (t, 0))] + [full(a) for a in consts],
        out_specs=[pl.BlockSpec((tt, width), lambda t: (t, 0)),
                   pl.BlockSpec((n_slab, 2 * S5_SLAB_STATE), lambda t: (0, 0))],
        out_shape=[jax.ShapeDtypeStruct((s, width), BF16),
                   jax.ShapeDtypeStruct((n_slab, 2 * S5_SLAB_STATE), F32)],
        scratch_shapes=[pltpu.VMEM((tt, 2 * S5_SLAB_STATE), F32),
                        pltpu.VMEM((tt, width), F32),
                        pltpu.VMEM((n_slab, 2 * S5_SLAB_STATE), F32)],
        compiler_params=_params(("arbitrary",)),
        name="s5_prompt",
    )(u, *consts)
    groups = n_slab * S5_SLAB_GROUPS
    s_re = st[:, :S5_SLAB_STATE].reshape(groups, S5_STATE)
    s_im = st[:, S5_SLAB_STATE:].reshape(groups, S5_STATE)
    return out, s_re, s_im


def _s5_sample_kernel(u_ref, sre_ref, sim_ref, win_ref, wout_ref, scan_ref, d_ref, wglu_ref,
                      bglu_ref, o_ref, xre_ref, xim_ref, y_scr):
    n_slab = win_ref.shape[0]
    ns = S5_SLAB_STATE
    for j in range(n_slab):
        lanes = slice(j * V7X_LANES, (j + 1) * V7X_LANES)
        st = slice(j * ns, (j + 1) * ns)
        u_j = u_ref[:, lanes]
        bu = jnp.dot(u_j.astype(BF16), win_ref[j], preferred_element_type=F32)
        ab_re, ab_im = scan_ref[j, 0, 0:1], scan_ref[j, 1, 0:1]
        s_re, s_im = sre_ref[:, st], sim_ref[:, st]
        xr = ab_re * s_re - ab_im * s_im + bu[:, 0:ns]
        xi = ab_re * s_im + ab_im * s_re + bu[:, ns:2 * ns]
        xre_ref[:, st] = xr
        xim_ref[:, st] = xi
        x = jnp.concatenate([xr, xi], axis=1).astype(BF16)
        y = jnp.dot(x, wout_ref[j], preferred_element_type=F32)
        y_scr[:, lanes] = y + d_ref[j] * u_j
    o_ref[...] = _glu(y_scr[...], wglu_ref, bglu_ref).astype(o_ref.dtype)


def _s5_sample(u, s_re, s_im, tabs, w_glu_b, b_glu):
    db, width = u.shape
    w_in, w_out, scan, dvec, _, _ = tabs
    n_state = s_re.shape[1] * s_re.shape[2]
    bglu = b_glu.reshape(1, width).astype(F32)
    args = (u, s_re.reshape(db, n_state).astype(F32), s_im.reshape(db, n_state).astype(F32),
            w_in, w_out, scan, dvec, w_glu_b, bglu)
    full = lambda a: pl.BlockSpec(a.shape, lambda i: (0,) * a.ndim)
    out, x_re, x_im = pl.pallas_call(
        _s5_sample_kernel,
        grid=(1,),
        in_specs=[full(a) for a in args],
        out_specs=[pl.BlockSpec((db, width), lambda i: (0, 0)),
                   pl.BlockSpec((db, n_state), lambda i: (0, 0)),
                   pl.BlockSpec((db, n_state), lambda i: (0, 0))],
        out_shape=[jax.ShapeDtypeStruct((db, width), BF16),
                   jax.ShapeDtypeStruct((db, n_state), F32),
                   jax.ShapeDtypeStruct((db, n_state), F32)],
        scratch_shapes=[pltpu.VMEM((db, width), F32)],
        compiler_params=_params(("arbitrary",)),
        name="s5_sample",
    )(*args)
    return out, x_re.reshape(s_re.shape), x_im.reshape(s_im.shape)


def _outproj_kernel(x_ref, a_ref, s_ref, w_ref, g_ref, h_ref, f_ref):
    half = a_ref.shape[1]
    h = (x_ref[...]
         + jnp.dot(a_ref[...], w_ref[0:half, :], preferred_element_type=F32)
         + jnp.dot(s_ref[...], w_ref[half:, :], preferred_element_type=F32))
    h_ref[...] = h
    ms = jnp.mean(h * h, axis=-1, keepdims=True)
    f_ref[...] = (h * lax.rsqrt(ms + EPS) * g_ref[...]).astype(BF16)


def _outproj(x, att, s5o, w_out_b, norm_ffn, tm):
    m, d = x.shape
    half = att.shape[1]
    tm = _tile(m, tm)
    row = lambda i: (i, 0)
    const = lambda i: (0, 0)
    return pl.pallas_call(
        _outproj_kernel,
        grid=(m // tm,),
        in_specs=[pl.BlockSpec((tm, d), row), pl.BlockSpec((tm, half), row),
                  pl.BlockSpec((tm, half), row), pl.BlockSpec((d, d), const),
                  pl.BlockSpec((1, d), const)],
        out_specs=[pl.BlockSpec((tm, d), row), pl.BlockSpec((tm, d), row)],
        out_shape=[jax.ShapeDtypeStruct((m, d), F32), jax.ShapeDtypeStruct((m, d), BF16)],
        compiler_params=_params(("parallel",)),
        name="out_proj",
    )(x, att, s5o, w_out_b, norm_ffn.reshape(1, d))


def _ffn_init(h_ref, o_ref):
    @pl.when(pl.program_id(1) == 0)
    def _():
        o_ref[...] = h_ref[...]


def _ffn_kernel(f_ref, h_ref, wg_ref, wu_ref, wd_ref, cw_ref, cb_ref,
                fs_ref, hs_ref, b0_ref, b1_ref,
                o_ref, buf_ref, os_ref, hgs_ref, hgx_ref, prev_ref, *, tm, tc):
    i = pl.program_id(0)
    j = pl.program_id(1)
    halo = V7X_SUBLANES
    tf = wg_ref.shape[1]
    _ffn_init(h_ref, o_ref)

    @pl.when(i == 0)
    def _():
        hgx_ref[0:halo, :] = jnp.zeros((halo, tf), F32)

    @pl.when(i > 0)
    def _():
        hgx_ref[0:halo, :] = prev_ref[j]

    f = f_ref[...]
    blocks = [slice(c0, c0 + tc) for c0 in range(0, tf, tc)]
    hgs = [None] * len(blocks)
    hus = [None] * len(blocks)

    def up(n):
        hgs[n] = jnp.dot(f, wg_ref[:, blocks[n]], preferred_element_type=F32)
        hus[n] = jnp.dot(f, wu_ref[:, blocks[n]], preferred_element_type=F32)

    up(0)
    part = None
    for n, cs in enumerate(blocks):
        if n + 1 < len(blocks):
            up(n + 1)
        hg = hgs[n]
        hgx_ref[halo:, cs] = hg
        cw = cw_ref[:, cs]
        conv = cb_ref[:, cs] + cw[2:3] * hg
        for tap in range(CONV_W - 1):
            off = halo - (CONV_W - 1) + tap
            conv = conv + cw[tap:tap + 1] * hgx_ref[pl.ds(off, tm), cs]
        act = (jax.nn.gelu(conv) * hus[n]).astype(BF16)
        down = jnp.dot(act, wd_ref[cs, :], preferred_element_type=F32)
        part = down if part is None else part + down
    tail = hgx_ref[tm:tm + halo, :]
    prev_ref[j] = tail
    buf_ref[...] = tail
    o_ref[...] += part

    @pl.when(i == 0)
    def _():
        fs = fs_ref[...]
        hg = jnp.dot(fs, wg_ref[...], preferred_element_type=F32)
        hu = jnp.dot(fs, wu_ref[...], preferred_element_type=F32)
        hgs_ref[...] = hg
        cw = cw_ref[...]
        conv = cb_ref[...] + cw[0:1] * b0_ref[...] + cw[1:2] * b1_ref[...] + cw[2:3] * hg
        down = jnp.dot((jax.nn.gelu(conv) * hu).astype(BF16), wd_ref[...],
                       preferred_element_type=F32)

        @pl.when(j == 0)
        def _():
            os_ref[...] = hs_ref[...] + down

        @pl.when(j > 0)
        def _():
            os_ref[...] += down


def _ffn(f, h, fs, hs, conv_buf_s, w_gate_b, w_up_b, w_down_b, conv_w, conv_b, tm, tf):
    m, d = f.shape
    ms = fs.shape[0]
    dff = w_gate_b.shape[1]
    tm = _tile(m, tm)
    tf = _tile(dff, tf)
    n_ff = dff // tf
    halo = V7X_SUBLANES
    row = lambda i, j: (i, 0)
    col = lambda i, j: (0, j)
    const = lambda i, j: (0, 0)
    hold = lambda i, j: (0, jnp.where(i == 0, j, n_ff - 1))
    b0 = conv_buf_s[:, 0, :].astype(F32)
    b1 = conv_buf_s[:, 1, :].astype(F32)
    kern = functools.partial(_ffn_kernel, tm=tm, tc=_tile(tf, V7X_MXU_COLS))
    out, tail, out_s, hg_s = pl.pallas_call(
        kern,
        grid=(m // tm, n_ff),
        in_specs=[pl.BlockSpec((tm, d), row), pl.BlockSpec((tm, d), row),
                  pl.BlockSpec((d, tf), col), pl.BlockSpec((d, tf), col),
                  pl.BlockSpec((tf, d), lambda i, j: (j, 0)),
                  pl.BlockSpec((CONV_W, tf), col), pl.BlockSpec((1, tf), col),
                  pl.BlockSpec((ms, d), const), pl.BlockSpec((ms, d), const),
                  pl.BlockSpec((ms, tf), col), pl.BlockSpec((ms, tf), col)],
        out_specs=[pl.BlockSpec((tm, d), row),
                   pl.BlockSpec((None, halo, tf), lambda i, j: (i, 0, j)),
                   pl.BlockSpec((ms, d), const),
                   pl.BlockSpec((ms, tf), hold)],
        out_shape=[jax.ShapeDtypeStruct((m, d), F32),
                   jax.ShapeDtypeStruct((m // tm, halo, dff), F32),
                   jax.ShapeDtypeStruct((ms, d), F32),
                   jax.ShapeDtypeStruct((ms, dff), F32)],
        scratch_shapes=[pltpu.VMEM((tm + halo, tf), F32),
                        pltpu.VMEM((n_ff, halo, tf), F32)],
        compiler_params=_params(("arbitrary", "arbitrary")),
        name="ffn",
    )(f, h, w_gate_b, w_up_b, w_down_b, conv_w.astype(F32), conv_b.reshape(1, dff).astype(F32),
      fs, hs, b0, b1)
    return out, tail[-1, halo - (CONV_W - 1):], out_s, jnp.stack([b1, hg_s], axis=1)


def _ple_kernel(h_ref, p_ref, wg_ref, wp_ref, g_ref, o_ref):
    h = h_ref[...]
    gate = jax.nn.sigmoid(jnp.dot(h.astype(BF16), wg_ref[...], preferred_element_type=F32))
    e = jnp.dot(p_ref[...].astype(BF16), wp_ref[...], preferred_element_type=F32)
    ms = jnp.mean(e * e, axis=-1, keepdims=True)
    o_ref[...] = h + gate * (e * lax.rsqrt(ms + EPS) * g_ref[...])


def _ple(h, p, w_gate_b, w_proj_b, ple_norm, tm):
    m, d = h.shape
    pd = p.shape[1]
    tm = _tile(m, tm)
    row = lambda i: (i, 0)
    const = lambda i: (0, 0)
    return pl.pallas_call(
        _ple_kernel,
        grid=(m // tm,),
        in_specs=[pl.BlockSpec((tm, d), row), pl.BlockSpec((tm, pd), row),
                  pl.BlockSpec((d, d), const), pl.BlockSpec((pd, d), const),
                  pl.BlockSpec((1, d), const)],
        out_specs=pl.BlockSpec((tm, d), row),
        out_shape=jax.ShapeDtypeStruct((m, d), F32),
        compiler_params=_params(("parallel",)),
        name="ple_gate",
    )(h, p, w_gate_b, w_proj_b, ple_norm.reshape(1, d))


def kernel(x_prompt, x_sample, cache_k, cache_v, state_s5_re, state_s5_im, state_conv, page_table, p_prompt, p_sample, norm_mix, w_in, q_norm, k_norm, lam_q1, lam_k1, lam_q2, lam_k2, subln, s5_a_re, s5_a_im, s5_b_re, s5_b_im, s5_c_re, s5_c_im, s5_d, s5_log_dt, w_glu, b_glu, w_out, norm_ffn, w_gate, w_up, conv_w, conv_b, w_down, w_ple_gate, w_ple_proj, ple_norm):
    depth = w_in.shape[0]
    b, s, d = x_prompt.shape
    db, t_new, _ = x_sample.shape
    assert b == 1 and t_new == 1
    page = cache_k.shape[2]
    past_len = page_table.shape[1] * page
    hp = x_prompt.reshape(s, d)
    hs = x_sample.reshape(db, d)
    pos_p = np.arange(s)
    pos_s = np.full((db,), past_len)
    n_pool = cache_k.shape[1]
    ck = jnp.transpose(cache_k, (0, 1, 3, 4, 5, 2)).reshape(depth * n_pool, -1, page)
    cv = cache_v.reshape(depth * n_pool, page * N_HEADS, V_DIM)
    outs = [[] for _ in range(10)]
    for i in range(depth):
        lam_init = 0.8 - 0.6 * math.exp(-0.3 * i)
        lam_vec = jnp.stack([lam_q1[i], lam_k1[i], lam_q2[i], lam_k2[i]]).astype(F32)
        w_in_b = w_in[i].astype(BF16)
        w_glu_b = w_glu[i].astype(BF16)
        w_out_b = w_out[i].astype(BF16)
        w_gate_b = w_gate[i].astype(BF16)
        w_up_b = w_up[i].astype(BF16)
        w_down_b = w_down[i].astype(BF16)
        w_pg_b = w_ple_gate[i].astype(BF16)
        w_pp_b = w_ple_proj[i].astype(BF16)
        tabs = _s5_tables(s5_a_re[i], s5_a_im[i], s5_b_re[i], s5_b_im[i], s5_c_re[i],
                          s5_c_im[i], s5_d[i], s5_log_dt[i], S5_TILE // V7X_SUBLANES)

        q, k32, kb, v32, vt, u = _project(hp, pos_p, norm_mix[i], w_in_b, q_norm[i], k_norm[i],
                                          ROW_TILE, True)
        qs, k32s, kbs, v32s, vbs, us = _project(hs, pos_s, norm_mix[i], w_in_b, q_norm[i],
                                                k_norm[i], ROW_TILE, False)
        att, att_s = _attention(q, kb, vt, qs, kbs, vbs, ck, cv, page_table + i * n_pool,
                                lam_vec, subln[i], lam_init, ATTN_TILE)

        s5o, s_re, s_im = _s5_prompt(u, tabs, w_glu_b, b_glu[i], S5_TILE)
        h1, f = _outproj(hp, att, s5o, w_out_b, norm_ffn[i], ROW_TILE)
        s5o_s, s_re_s, s_im_s = _s5_sample(us, state_s5_re[i], state_s5_im[i], tabs, w_glu_b,
                                           b_glu[i])
        h1s, f_s = _outproj(hs, att_s, s5o_s, w_out_b, norm_ffn[i], ROW_TILE)
        h2, cbuf, h2s, cbuf_s = _ffn(f, h1, f_s, h1s, state_conv[i], w_gate_b, w_up_b, w_down_b,
                                     conv_w[i], conv_b[i], ROW_TILE, FFN_COL_TILE)
        hp = _ple(h2, p_prompt[i].reshape(s, -1), w_pg_b, w_pp_b, ple_norm[i], ROW_TILE)
        outs[0].append(k32.reshape(b, s, N_HEADS, 2, HEAD_DIM))
        outs[1].append(v32.reshape(b, s, N_HEADS, V_DIM))
        outs[2].append(s_re[None])
        outs[3].append(s_im[None])
        outs[4].append(cbuf[None])

        hs = _ple(h2s, p_sample[i].reshape(db, -1), w_pg_b, w_pp_b, ple_norm[i], ROW_TILE)
        outs[5].append(k32s.reshape(db, t_new, N_HEADS, 2, HEAD_DIM))
        outs[6].append(v32s.reshape(db, t_new, N_HEADS, V_DIM))
        outs[7].append(s_re_s)
        outs[8].append(s_im_s)
        outs[9].append(cbuf_s)
    return (hp.reshape(b, s, d), hs.reshape(db, t_new, d)) + tuple(jnp.stack(o) for o in outs)
```
